```python
import math
import jax, jax.numpy as jnp
from jax import lax
import numpy as np

D_MODEL = 1024
BATCH = 2
SEQ = 8192
DEPTH = 2

CHUNK = 64
N_MIXERS = 2
N_A = (DEPTH + 1) // 2
N_B = DEPTH // 2

D_RNN = D_MODEL
RG_BLOCKS = 4
RG_BW = D_RNN // RG_BLOCKS
CONV_W = 4
RG_C = 8.0

SB_HEADS = 16
SB_HEAD_DIM = D_MODEL // SB_HEADS
Q_BLOCK = 128

D_FF = int(math.ceil(8 * D_MODEL / 3 / 256) * 256)

RMS_EPS = 1e-6

kernel_name = "hybrid_rglru_stickbreaking_trunk"


def _rmsnorm(x, g):
    x32 = x.astype(jnp.float32)
    y = x32 * lax.rsqrt(jnp.mean(x32 * x32, axis=-1, keepdims=True) + RMS_EPS)
    return (y * g.astype(jnp.float32)).astype(x.dtype)


def _causal_depthwise_conv(x, w, b):
    c = x.shape[-1]
    y = lax.conv_general_dilated(
        x, w.astype(x.dtype)[:, None, :], window_strides=(1,),
        padding=[(CONV_W - 1, 0)], dimension_numbers=("NWC", "WIO", "NWC"),
        feature_group_count=c)
    return y + b.astype(x.dtype)


def _linear_scan(a, u):
    def combine(l, r):
        a_l, b_l = l
        a_r, b_r = r
        return a_l * a_r, a_r * b_l + b_r
    _, h = lax.associative_scan(combine, (a, u), axis=1)
    return h


def _rglru_mixer(h, w_in, conv_w, conv_b, w_r, b_r, w_i, b_i, lam, w_out):
    bsz, s, _ = h.shape
    proj = h @ w_in.astype(h.dtype)
    gate_br, x_br = proj[..., :D_RNN], proj[..., D_RNN:]
    gate = jax.nn.gelu(gate_br, approximate=True)
    xc = _causal_depthwise_conv(x_br, conv_w, conv_b)
    xg = xc.reshape(bsz, s, RG_BLOCKS, RG_BW).astype(jnp.float32)
    r = jax.nn.sigmoid(jnp.einsum("bsnc,ncd->bsnd", xg, w_r.astype(jnp.float32)).reshape(bsz, s, D_RNN)
                       + b_r.astype(jnp.float32))
    i = jax.nn.sigmoid(jnp.einsum("bsnc,ncd->bsnd", xg, w_i.astype(jnp.float32)).reshape(bsz, s, D_RNN)
                       + b_i.astype(jnp.float32))
    log_a = RG_C * r * jax.nn.log_sigmoid(lam.astype(jnp.float32))
    a = jnp.exp(log_a)
    mult = jnp.sqrt(jnp.maximum(-jnp.expm1(2.0 * log_a), 0.0))
    u = mult * (i * xc.astype(jnp.float32))
    hs = _linear_scan(a, u)
    y = (hs * gate.astype(jnp.float32)).astype(h.dtype)
    return y @ w_out.astype(h.dtype)


def _stick_breaking_attention(q, k, v):
    bsz, nh, s, dh = q.shape
    nq = s // Q_BLOCK
    scale = 1.0 / math.sqrt(dh)
    k32 = k.astype(jnp.float32)
    v32 = v.astype(jnp.float32)
    key_pos = jnp.arange(s)
    q_blocks = q.reshape(bsz, nh, nq, Q_BLOCK, dh).transpose(2, 0, 1, 3, 4)
    starts = jnp.arange(nq) * Q_BLOCK

    def block(args):
        qb, start = args
        z = jnp.einsum("bhqd,bhkd->bhqk", qb.astype(jnp.float32), k32) * scale
        q_pos = start + jnp.arange(Q_BLOCK)
        mask = key_pos[None, :] < q_pos[:, None]
        log_beta = jax.nn.log_sigmoid(z)
        log_1m = jnp.where(mask, jax.nn.log_sigmoid(-z), 0.0)
        suffix = lax.cumsum(log_1m, axis=3, reverse=True) - log_1m
        wts = jnp.where(mask, jnp.exp(log_beta + suffix), 0.0)
        return jnp.einsum("bhqk,bhkd->bhqd", wts, v32)

    out = lax.map(block, (q_blocks, starts))
    return out.transpose(1, 2, 0, 3, 4).reshape(bsz, nh, s, dh).astype(q.dtype)


def _sb_mixer(h, w_qkv, w_out):
    bsz, s, _ = h.shape
    qkv = (h @ w_qkv.astype(h.dtype)).reshape(bsz, s, 3, SB_HEADS, SB_HEAD_DIM)
    qkv = qkv.transpose(2, 0, 3, 1, 4)
    o = _stick_breaking_attention(qkv[0], qkv[1], qkv[2])
    o = o.transpose(0, 2, 1, 3).reshape(bsz, s, D_MODEL)
    return o @ w_out.astype(h.dtype)


def _swiglu(h, w_gate, w_up, w_down):
    g = h @ w_gate.astype(h.dtype)
    u = h @ w_up.astype(h.dtype)
    return (jax.nn.silu(g) * u) @ w_down.astype(h.dtype)


def setup_inputs(seed: int = 0) -> dict:
    key = jax.random.key(seed)
    ks = jax.random.split(key, 20)
    f32 = jnp.float32

    def nrm(k, shape, fan_in):
        return jax.random.normal(k, shape, f32) * (fan_in ** -0.5)

    a0 = jax.random.uniform(ks[10], (N_A, D_RNN), f32, 0.9, 0.999)
    base = a0 ** (1.0 / RG_C)
    lam = jnp.log(base) - jnp.log1p(-base)

    return {
        "x": jax.random.normal(ks[0], (BATCH, SEQ, D_MODEL), f32),
        "norm_mix_g": 1.0 + 0.02 * jax.random.normal(ks[1], (DEPTH, D_MODEL), f32),
        "norm_ffn_g": 1.0 + 0.02 * jax.random.normal(ks[2], (DEPTH, D_MODEL), f32),
        "a_w_in": nrm(ks[3], (N_A, D_MODEL, 2 * D_RNN), D_MODEL),
        "a_conv_w": nrm(ks[4], (N_A, CONV_W, D_RNN), CONV_W),
        "a_conv_b": 0.01 * jax.random.normal(ks[5], (N_A, D_RNN), f32),
        "a_w_r": nrm(ks[6], (N_A, RG_BLOCKS, RG_BW, RG_BW), RG_BW),
        "a_b_r": 0.01 * jax.random.normal(ks[7], (N_A, D_RNN), f32),
        "a_w_i": nrm(ks[8], (N_A, RG_BLOCKS, RG_BW, RG_BW), RG_BW),
        "a_b_i": 0.01 * jax.random.normal(ks[9], (N_A, D_RNN), f32),
        "a_lambda": lam,
        "a_w_out": nrm(ks[11], (N_A, D_RNN, D_MODEL), D_RNN),
        "b_w_qkv": nrm(ks[12], (N_B, D_MODEL, 3 * D_MODEL), D_MODEL),
        "b_w_out": nrm(ks[13], (N_B, D_MODEL, D_MODEL), D_MODEL),
        "ffn_w_gate": nrm(ks[14], (DEPTH, D_MODEL, D_FF), D_MODEL),
        "ffn_w_up": nrm(ks[15], (DEPTH, D_MODEL, D_FF), D_MODEL),
        "ffn_w_down": nrm(ks[16], (DEPTH, D_FF, D_MODEL), D_FF),
        "final_g": 1.0 + 0.02 * jax.random.normal(ks[17], (D_MODEL,), f32),
    }


def reference(x, norm_mix_g, norm_ffn_g, a_w_in, a_conv_w, a_conv_b, a_w_r, a_b_r,
              a_w_i, a_b_i, a_lambda, a_w_out, b_w_qkv, b_w_out,
              ffn_w_gate, ffn_w_up, ffn_w_down, final_g):
    for layer in range(DEPTH):
        h = _rmsnorm(x, norm_mix_g[layer])
        if layer % N_MIXERS == 0:
            j = layer // N_MIXERS
            mix = _rglru_mixer(h, a_w_in[j], a_conv_w[j], a_conv_b[j], a_w_r[j], a_b_r[j],
                               a_w_i[j], a_b_i[j], a_lambda[j], a_w_out[j])
        else:
            j = layer // N_MIXERS
            mix = _sb_mixer(h, b_w_qkv[j], b_w_out[j])
        x = x + mix
        h = _rmsnorm(x, norm_ffn_g[layer])
        x = x + _swiglu(h, ffn_w_gate[layer], ffn_w_up[layer], ffn_w_down[layer])
    return _rmsnorm(x, final_g)
```

```python
import functools
import math

import jax
import jax.numpy as jnp
from jax import lax
from jax.experimental import pallas as pl
from jax.experimental.pallas import tpu as pltpu

RMS_EPS = 1e-6
RG_C = 8.0
CONV_W = 4
RG_BLOCKS = 4
SB_HEAD_DIM = 64
LANES = 128
SUBLANES = 8
VMEM_LIMIT = 48 * 1024 * 1024

BF16 = jnp.bfloat16
F32 = jnp.float32


def _rmsnorm(x, g):
    ms = jnp.mean(x * x, axis=-1, keepdims=True)
    return x * lax.rsqrt(ms + RMS_EPS) * g


def _dot(a, b):
    return jnp.dot(a, b, preferred_element_type=F32)


def _softplus(z):
    return jnp.maximum(z, 0.0) + jnp.log(1.0 + jnp.exp(-jnp.abs(z)))


def _rglru_kernel(x_ref, g_ref, win_ref, cw_ref, cb_ref, wr_ref, br_ref, wi_ref, bi_ref,
                  lam_ref, wout_ref, o_ref, xpad_ref, a_ref, u_ref, hs_ref, h_ref):
    tm, d = x_ref.shape
    bw = d // RG_BLOCKS
    pad = SUBLANES

    @pl.when(pl.program_id(1) == 0)
    def _():
        xpad_ref[0:pad, :] = jnp.zeros((pad, d), F32)
        h_ref[...] = jnp.zeros_like(h_ref)

    x = x_ref[...]
    hb = _rmsnorm(x, g_ref[...]).astype(BF16)
    proj = _dot(hb, win_ref[...])
    gate = jax.nn.gelu(proj[:, :d], approximate=True)
    xpad_ref[pad:pad + tm, :] = proj[:, d:]

    xc = cb_ref[...] + cw_ref[CONV_W - 1:CONV_W, :] * xpad_ref[pad:pad + tm, :]
    for k in range(CONV_W - 1):
        off = pad - (CONV_W - 1) + k
        xc = xc + cw_ref[k:k + 1, :] * xpad_ref[off:off + tm, :]
    xpad_ref[0:pad, :] = xpad_ref[tm:tm + pad, :]

    xcb = xc.astype(BF16)
    lam = lam_ref[...]
    log_sig_lam = -_softplus(-lam)
    for n in range(RG_BLOCKS):
        sl = slice(n * bw, (n + 1) * bw)
        xb = xcb[:, sl]
        r = jax.nn.sigmoid(_dot(xb, wr_ref[n]) + br_ref[:, sl])
        i = jax.nn.sigmoid(_dot(xb, wi_ref[n]) + bi_ref[:, sl])
        log_a = RG_C * r * log_sig_lam[:, sl]
        a = jnp.exp(log_a)
        mult = jnp.sqrt(jnp.maximum(-jnp.tanh(log_a) * (a * a + 1.0), 0.0))
        a_ref[:, sl] = a
        u_ref[:, sl] = mult * (i * xc[:, sl])

    row = lax.broadcasted_iota(jnp.int32, (SUBLANES, d), 0)

    def scan_group(gi, h):
        r0 = pl.multiple_of(gi * SUBLANES, SUBLANES)
        a = a_ref[pl.ds(r0, SUBLANES), :]
        u = u_ref[pl.ds(r0, SUBLANES), :]
        s = 1
        while s < SUBLANES:
            keep = row >= s
            a_sh = jnp.where(keep, pltpu.roll(a, s, 0), 1.0)
            u_sh = jnp.where(keep, pltpu.roll(u, s, 0), 0.0)
            u = a * u_sh + u
            a = a * a_sh
            s *= 2
        hs = a * h + u
        hs_ref[pl.ds(r0, SUBLANES), :] = hs
        return hs[SUBLANES - 1:SUBLANES, :]

    h_last = lax.fori_loop(0, tm // SUBLANES, scan_group, h_ref[...])
    h_ref[...] = h_last

    y = (hs_ref[...] * gate).astype(BF16)
    o_ref[...] = x + _dot(y, wout_ref[...])


def _rglru_mixer(x, g, w_in, conv_w, conv_b, w_r, b_r, w_i, b_i, lam, w_out, *, tm):
    bsz, s, d = x.shape
    bw = d // RG_BLOCKS
    const2 = lambda b, i: (0, 0)
    const3 = lambda b, i: (0, 0, 0)
    vec = pl.BlockSpec((1, d), const2)
    return pl.pallas_call(
        _rglru_kernel,
        grid=(bsz, s // tm),
        in_specs=[
            pl.BlockSpec((None, tm, d), lambda b, i: (b, i, 0)),
            vec,
            pl.BlockSpec((d, 2 * d), const2),
            pl.BlockSpec((CONV_W, d), const2),
            vec,
            pl.BlockSpec((RG_BLOCKS, bw, bw), const3),
            vec,
            pl.BlockSpec((RG_BLOCKS, bw, bw), const3),
            vec,
            vec,
            pl.BlockSpec((d, d), const2),
        ],
        out_specs=pl.BlockSpec((None, tm, d), lambda b, i: (b, i, 0)),
        out_shape=jax.ShapeDtypeStruct((bsz, s, d), F32),
        scratch_shapes=[
            pltpu.VMEM((tm + SUBLANES, d), F32),
            pltpu.VMEM((tm, d), F32),
            pltpu.VMEM((tm, d), F32),
            pltpu.VMEM((tm, d), F32),
            pltpu.VMEM((1, d), F32),
        ],
        compiler_params=pltpu.CompilerParams(
            dimension_semantics=("arbitrary", "arbitrary"),
            vmem_limit_bytes=VMEM_LIMIT),
        name="rglru_mixer",
    )(x, g, w_in, conv_w, conv_b, w_r, b_r, w_i, b_i, lam, w_out)


def _ffn_kernel(*refs, pre_proj, final_norm):
    refs = list(refs)
    x_ref = refs.pop(0)
    if pre_proj:
        att_ref = refs.pop(0)
        wo_ref = refs.pop(0)
    g_ref, wg_ref, wu_ref, wd_ref = refs[:4]
    refs = refs[4:]
    if final_norm:
        fg_ref = refs.pop(0)
    o_ref, xm_ref, hb_ref, acc_ref = refs

    j = pl.program_id(1)

    @pl.when(j == 0)
    def _():
        x = x_ref[...]
        if pre_proj:
            x = x + _dot(att_ref[...], wo_ref[...])
        xm_ref[...] = x
        hb_ref[...] = _rmsnorm(x, g_ref[...]).astype(BF16)
        acc_ref[...] = jnp.zeros_like(acc_ref)

    h = hb_ref[...]
    gate = _dot(h, wg_ref[...])
    up = _dot(h, wu_ref[...])
    act = (gate * jax.nn.sigmoid(gate) * up).astype(BF16)
    acc_ref[...] += _dot(act, wd_ref[...])

    @pl.when(j == pl.num_programs(1) - 1)
    def _():
        y = xm_ref[...] + acc_ref[...]
        if final_norm:
            y = _rmsnorm(y, fg_ref[...])
        o_ref[...] = y


def _ffn(x, g, w_gate, w_up, w_down, *, att=None, w_o=None, final_g=None, tm, tf):
    m, d = x.shape
    dff = w_gate.shape[1]
    pre_proj = att is not None
    final_norm = final_g is not None
    row = pl.BlockSpec((tm, d), lambda i, j: (i, 0))
    vec = pl.BlockSpec((1, d), lambda i, j: (0, 0))
    args, specs = [x], [row]
    if pre_proj:
        args += [att, w_o]
        specs += [row, pl.BlockSpec((d, d), lambda i, j: (0, 0))]
    args += [g, w_gate, w_up, w_down]
    specs += [vec,
              pl.BlockSpec((d, tf), lambda i, j: (0, j)),
              pl.BlockSpec((d, tf), lambda i, j: (0, j)),
              pl.BlockSpec((tf, d), lambda i, j: (j, 0))]
    if final_norm:
        args.append(final_g)
        specs.append(vec)
    return pl.pallas_call(
        functools.partial(_ffn_kernel, pre_proj=pre_proj, final_norm=final_norm),
        grid=(m // tm, dff // tf),
        in_specs=specs,
        out_specs=row,
        out_shape=jax.ShapeDtypeStruct((m, d), F32),
        scratch_shapes=[
            pltpu.VMEM((tm, d), F32),
            pltpu.VMEM((tm, d), BF16),
            pltpu.VMEM((tm, d), F32),
        ],
        compiler_params=pltpu.CompilerParams(
            dimension_semantics=("arbitrary", "arbitrary"),
            vmem_limit_bytes=VMEM_LIMIT),
        name="ffn_attn_out" if pre_proj else "ffn",
    )(*args)


def _qkv_kernel(x_ref, g_ref, w_ref, o_ref, hb_ref, *, q_scale):
    j = pl.program_id(1)

    @pl.when(j == 0)
    def _():
        hb_ref[...] = _rmsnorm(x_ref[...], g_ref[...]).astype(BF16)

    scale = jnp.where(j == 0, q_scale, 1.0).astype(F32)
    o_ref[...] = (_dot(hb_ref[...], w_ref[...]) * scale).astype(o_ref.dtype)


def _qkv_proj(x, g, w_qkv, *, tm):
    m, d = x.shape
    n = w_qkv.shape[1]
    return pl.pallas_call(
        functools.partial(_qkv_kernel, q_scale=1.0 / math.sqrt(SB_HEAD_DIM)),
        grid=(m // tm, n // d),
        in_specs=[
            pl.BlockSpec((tm, d), lambda i, j: (i, 0)),
            pl.BlockSpec((1, d), lambda i, j: (0, 0)),
            pl.BlockSpec((d, d), lambda i, j: (0, j)),
        ],
        out_specs=pl.BlockSpec((tm, d), lambda i, j: (i, j)),
        out_shape=jax.ShapeDtypeStruct((m, n), BF16),
        scratch_shapes=[pltpu.VMEM((tm, d), BF16)],
        compiler_params=pltpu.CompilerParams(
            dimension_semantics=("arbitrary", "arbitrary"),
            vmem_limit_bytes=VMEM_LIMIT),
        name="qkv_proj",
    )(x, g, w_qkv)


def _sb_attn_kernel(q_ref, k_ref, v_ref, o_ref):
    t = q_ref.shape[0]
    i = pl.program_id(2)

    lane_q = lax.broadcasted_iota(jnp.int32, (t, LANES), 1)
    q = q_ref[...]
    zero = jnp.zeros_like(q)
    qs = jnp.concatenate([jnp.where(lane_q < SB_HEAD_DIM, q, zero),
                          jnp.where(lane_q >= SB_HEAD_DIM, q, zero)], axis=0)

    kr = lax.broadcasted_iota(jnp.int32, (t, t), 0)
    kc = lax.broadcasted_iota(jnp.int32, (t, t), 1)
    tri = jnp.where(kr > kc, -1.0, 0.0).astype(BF16)
    qrow = lax.broadcasted_iota(jnp.int32, (2 * t, t), 0)
    qrow = jnp.where(qrow >= t, qrow - t, qrow)
    kcol = lax.broadcasted_iota(jnp.int32, (2 * t, t), 1)
    causal = kcol < qrow
    lane_v = lax.broadcasted_iota(jnp.int32, (t, LANES), 1)

    def block(j, c, masked):
        r0 = pl.multiple_of(j * t, t)
        k = k_ref[pl.ds(r0, t), :]
        v = v_ref[pl.ds(r0, t), :]
        z = lax.dot_general(qs, k, (((1,), (1,)), ((), ())), preferred_element_type=F32)
        sp = _softplus(z)
        log_beta = z - sp
        if masked:
            sp = jnp.where(causal, sp, 0.0)
        hi = sp.astype(BF16)
        lo = (sp - hi.astype(F32)).astype(BF16)
        sfx = _dot(hi, tri) + _dot(lo, tri)
        w = jnp.exp(log_beta + sfx)
        if masked:
            w = jnp.where(causal, w, 0.0)
        w = w.astype(BF16)
        zv = jnp.zeros_like(v)
        pv0 = _dot(w[:t], jnp.where(lane_v < SB_HEAD_DIM, v, zv))
        pv1 = _dot(w[t:], jnp.where(lane_v >= SB_HEAD_DIM, v, zv))
        ec = jnp.exp(c)
        contrib = ec[:t] * pv0 + ec[t:] * pv1
        c_new = c + sfx[:, 0:1] - sp[:, 0:1]
        return contrib, c_new

    acc, c = block(i, jnp.zeros((2 * t, 1), F32), True)

    def body(jj, carry):
        acc, c = carry
        contrib, c = block(i - 1 - jj, c, False)
        return acc + contrib, c

    acc, c = lax.fori_loop(0, i, body, (acc, c))
    o_ref[...] = acc.astype(o_ref.dtype)


def _sb_attention(qkv, *, d, t):
    bsz, s, _ = qkv.shape
    pairs = d // LANES
    return pl.pallas_call(
        _sb_attn_kernel,
        grid=(bsz, pairs, s // t),
        in_specs=[
            pl.BlockSpec((None, t, LANES), lambda b, p, i: (b, i, p)),
            pl.BlockSpec((None, s, LANES), lambda b, p, i: (b, 0, pairs + p)),
            pl.BlockSpec((None, s, LANES), lambda b, p, i: (b, 0, 2 * pairs + p)),
        ],
        out_specs=pl.BlockSpec((None, t, LANES), lambda b, p, i: (b, i, p)),
        out_shape=jax.ShapeDtypeStruct((bsz, s, d), BF16),
        compiler_params=pltpu.CompilerParams(
            dimension_semantics=("arbitrary", "arbitrary", "arbitrary"),
            vmem_limit_bytes=VMEM_LIMIT),
        name="sb_attention",
    )(qkv, qkv, qkv)


def kernel(x, norm_mix_g, norm_ffn_g, a_w_in, a_conv_w, a_conv_b, a_w_r, a_b_r, a_w_i, a_b_i,
           a_lambda, a_w_out, b_w_qkv, b_w_out, ffn_w_gate, ffn_w_up, ffn_w_down, final_g):
    bsz, s, d = x.shape
    m = bsz * s
    row = lambda v: v.reshape(1, -1)
    bf = lambda w: w.astype(BF16)

    x = _rglru_mixer(x, row(norm_mix_g[0]), bf(a_w_in[0]), a_conv_w[0], row(a_conv_b[0]),
                     bf(a_w_r[0]), row(a_b_r[0]), bf(a_w_i[0]), row(a_b_i[0]),
                     row(a_lambda[0]), bf(a_w_out[0]), tm=256)
    x = _ffn(x.reshape(m, d), row(norm_ffn_g[0]), bf(ffn_w_gate[0]), bf(ffn_w_up[0]),
             bf(ffn_w_down[0]), tm=512, tf=256)

    qkv = _qkv_proj(x, row(norm_mix_g[1]), bf(b_w_qkv[0]), tm=512)
    att = _sb_attention(qkv.reshape(bsz, s, 3 * d), d=d, t=256)
    out = _ffn(x, row(norm_ffn_g[1]), bf(ffn_w_gate[1]), bf(ffn_w_up[1]), bf(ffn_w_down[1]),
               att=att.reshape(m, d), w_o=bf(b_w_out[0]), final_g=row(final_g), tm=512, tf=256)
    return out.reshape(bsz, s, d)
```

```python
import functools
import math

import jax
import jax.numpy as jnp
from jax import lax
from jax.experimental import pallas as pl
from jax.experimental.pallas import tpu as pltpu

RMS_EPS = 1e-6
RG_C = 8.0
CONV_W = 4
RG_BLOCKS = 4
SB_HEAD_DIM = 64
LANES = 128
SUBLANES = 8
Q_TILE = 128
KEY_BLOCK = 256
LOG2E = 1.4426950408889634
EXIT_LOG = -90.0
VMEM_LIMIT = 48 * 1024 * 1024

BF16 = jnp.bfloat16
F32 = jnp.float32


def _rmsnorm(x, g):
    ms = jnp.mean(x * x, axis=-1, keepdims=True)
    return x * lax.rsqrt(ms + RMS_EPS) * g


def _dot(a, b):
    return jnp.dot(a, b, preferred_element_type=F32)


def _softplus(z):
    return jnp.maximum(z, 0.0) + jnp.log(1.0 + jnp.exp(-jnp.abs(z)))


def _rglru_kernel(x_ref, g_ref, win_ref, cw_ref, cb_ref, wr_ref, br_ref, wi_ref, bi_ref,
                  lam_ref, wout_ref, o_ref, xpad_ref, a_ref, u_ref, hs_ref, h_ref):
    tm, d = x_ref.shape
    bw = d // RG_BLOCKS
    pad = SUBLANES

    @pl.when(pl.program_id(1) == 0)
    def _():
        xpad_ref[0:pad, :] = jnp.zeros((pad, d), F32)
        h_ref[...] = jnp.zeros_like(h_ref)

    x = x_ref[...]
    hb = _rmsnorm(x, g_ref[...]).astype(BF16)
    proj = _dot(hb, win_ref[...])
    gate = jax.nn.gelu(proj[:, :d], approximate=True)
    xpad_ref[pad:pad + tm, :] = proj[:, d:]

    xc = cb_ref[...] + cw_ref[CONV_W - 1:CONV_W, :] * xpad_ref[pad:pad + tm, :]
    for k in range(CONV_W - 1):
        off = pad - (CONV_W - 1) + k
        xc = xc + cw_ref[k:k + 1, :] * xpad_ref[off:off + tm, :]
    xpad_ref[0:pad, :] = xpad_ref[tm:tm + pad, :]

    xcb = xc.astype(BF16)
    lam = lam_ref[...]
    log_sig_lam = -_softplus(-lam)
    for n in range(RG_BLOCKS):
        sl = slice(n * bw, (n + 1) * bw)
        xb = xcb[:, sl]
        r = jax.nn.sigmoid(_dot(xb, wr_ref[n]) + br_ref[:, sl])
        i = jax.nn.sigmoid(_dot(xb, wi_ref[n]) + bi_ref[:, sl])
        log_a = RG_C * r * log_sig_lam[:, sl]
        a = jnp.exp(log_a)
        mult = jnp.sqrt(jnp.maximum(-jnp.tanh(log_a) * (a * a + 1.0), 0.0))
        a_ref[:, sl] = a
        u_ref[:, sl] = mult * (i * xc[:, sl])

    row = lax.broadcasted_iota(jnp.int32, (SUBLANES, d), 0)

    def scan_group(gi, h):
        r0 = pl.multiple_of(gi * SUBLANES, SUBLANES)
        a = a_ref[pl.ds(r0, SUBLANES), :]
        u = u_ref[pl.ds(r0, SUBLANES), :]
        s = 1
        while s < SUBLANES:
            keep = row >= s
            a_sh = jnp.where(keep, pltpu.roll(a, s, 0), 1.0)
            u_sh = jnp.where(keep, pltpu.roll(u, s, 0), 0.0)
            u = a * u_sh + u
            a = a * a_sh
            s *= 2
        hs = a * h + u
        hs_ref[pl.ds(r0, SUBLANES), :] = hs
        return hs[SUBLANES - 1:SUBLANES, :]

    h_last = lax.fori_loop(0, tm // SUBLANES, scan_group, h_ref[...])
    h_ref[...] = h_last

    y = (hs_ref[...] * gate).astype(BF16)
    o_ref[...] = x + _dot(y, wout_ref[...])


def _rglru_mixer(x, g, w_in, conv_w, conv_b, w_r, b_r, w_i, b_i, lam, w_out, *, tm):
    bsz, s, d = x.shape
    bw = d // RG_BLOCKS
    const2 = lambda b, i: (0, 0)
    const3 = lambda b, i: (0, 0, 0)
    vec = pl.BlockSpec((1, d), const2)
    return pl.pallas_call(
        _rglru_kernel,
        grid=(bsz, s // tm),
        in_specs=[
            pl.BlockSpec((None, tm, d), lambda b, i: (b, i, 0)),
            vec,
            pl.BlockSpec((d, 2 * d), const2),
            pl.BlockSpec((CONV_W, d), const2),
            vec,
            pl.BlockSpec((RG_BLOCKS, bw, bw), const3),
            vec,
            pl.BlockSpec((RG_BLOCKS, bw, bw), const3),
            vec,
            vec,
            pl.BlockSpec((d, d), const2),
        ],
        out_specs=pl.BlockSpec((None, tm, d), lambda b, i: (b, i, 0)),
        out_shape=jax.ShapeDtypeStruct((bsz, s, d), F32),
        scratch_shapes=[
            pltpu.VMEM((tm + SUBLANES, d), F32),
            pltpu.VMEM((tm, d), F32),
            pltpu.VMEM((tm, d), F32),
            pltpu.VMEM((tm, d), F32),
            pltpu.VMEM((1, d), F32),
        ],
        compiler_params=pltpu.CompilerParams(
            dimension_semantics=("arbitrary", "arbitrary"),
            vmem_limit_bytes=VMEM_LIMIT),
        name="rglru_mixer",
    )(x, g, w_in, conv_w, conv_b, w_r, b_r, w_i, b_i, lam, w_out)


def _ffn_kernel(*refs, pre_proj, final_norm):
    refs = list(refs)
    x_ref = refs.pop(0)
    if pre_proj:
        att_ref = refs.pop(0)
        wo_ref = refs.pop(0)
    g_ref, wg_ref, wu_ref, wd_ref = refs[:4]
    refs = refs[4:]
    if final_norm:
        fg_ref = refs.pop(0)
    o_ref, xm_ref, hb_ref, acc_ref = refs

    j = pl.program_id(1)

    @pl.when(j == 0)
    def _():
        x = x_ref[...]
        if pre_proj:
            x = x + _dot(att_ref[...], wo_ref[...])
        xm_ref[...] = x
        hb_ref[...] = _rmsnorm(x, g_ref[...]).astype(BF16)
        acc_ref[...] = jnp.zeros_like(acc_ref)

    h = hb_ref[...]
    gate = _dot(h, wg_ref[...])
    up = _dot(h, wu_ref[...])
    act = (gate * jax.nn.sigmoid(gate) * up).astype(BF16)
    acc_ref[...] += _dot(act, wd_ref[...])

    @pl.when(j == pl.num_programs(1) - 1)
    def _():
        y = xm_ref[...] + acc_ref[...]
        if final_norm:
            y = _rmsnorm(y, fg_ref[...])
        o_ref[...] = y


def _ffn(x, g, w_gate, w_up, w_down, *, att=None, w_o=None, final_g=None, tm, tf):
    m, d = x.shape
    dff = w_gate.shape[1]
    pre_proj = att is not None
    final_norm = final_g is not None
    row = pl.BlockSpec((tm, d), lambda i, j: (i, 0))
    vec = pl.BlockSpec((1, d), lambda i, j: (0, 0))
    args, specs = [x], [row]
    if pre_proj:
        args += [att, w_o]
        specs += [row, pl.BlockSpec((d, d), lambda i, j: (0, 0))]
    args += [g, w_gate, w_up, w_down]
    specs += [vec,
              pl.BlockSpec((d, tf), lambda i, j: (0, j)),
              pl.BlockSpec((d, tf), lambda i, j: (0, j)),
              pl.BlockSpec((tf, d), lambda i, j: (j, 0))]
    if final_norm:
        args.append(final_g)
        specs.append(vec)
    return pl.pallas_call(
        functools.partial(_ffn_kernel, pre_proj=pre_proj, final_norm=final_norm),
        grid=(m // tm, dff // tf),
        in_specs=specs,
        out_specs=row,
        out_shape=jax.ShapeDtypeStruct((m, d), F32),
        scratch_shapes=[
            pltpu.VMEM((tm, d), F32),
            pltpu.VMEM((tm, d), BF16),
            pltpu.VMEM((tm, d), F32),
        ],
        compiler_params=pltpu.CompilerParams(
            dimension_semantics=("arbitrary", "arbitrary"),
            vmem_limit_bytes=VMEM_LIMIT),
        name="ffn_attn_out" if pre_proj else "ffn",
    )(*args)


def _qkv_kernel(x_ref, g_ref, w_ref, o_ref, hb_ref, *, q_scale):
    i = pl.program_id(1)
    j = pl.program_id(2)

    @pl.when(i == 0)
    def _():
        o_ref[...] = jnp.zeros_like(o_ref)

    @pl.when((i > 0) & (j == 0))
    def _():
        hb_ref[...] = _rmsnorm(x_ref[...], g_ref[...]).astype(BF16)

    @pl.when(i > 0)
    def _():
        scale = jnp.where(j == 0, q_scale, 1.0).astype(F32)
        o_ref[...] = (_dot(hb_ref[...], w_ref[...]) * scale).astype(o_ref.dtype)


def _qkv_proj(x, g, w_qkv, *, tm):
    bsz, s, d = x.shape
    n = w_qkv.shape[1]
    return pl.pallas_call(
        functools.partial(_qkv_kernel, q_scale=1.0 / math.sqrt(SB_HEAD_DIM)),
        grid=(bsz, s // tm + 1, n // d),
        in_specs=[
            pl.BlockSpec((None, tm, d), lambda b, i, j: (b, jnp.maximum(i - 1, 0), 0)),
            pl.BlockSpec((1, d), lambda b, i, j: (0, 0)),
            pl.BlockSpec((d, d), lambda b, i, j: (0, j)),
        ],
        out_specs=pl.BlockSpec((None, tm, d), lambda b, i, j: (b, i, j)),
        out_shape=jax.ShapeDtypeStruct((bsz, s + tm, n), BF16),
        scratch_shapes=[pltpu.VMEM((tm, d), BF16)],
        compiler_params=pltpu.CompilerParams(
            dimension_semantics=("arbitrary", "arbitrary", "arbitrary"),
            vmem_limit_bytes=VMEM_LIMIT),
        name="qkv_proj",
    )(x, g, w_qkv)


def _sb_attn_kernel(q_ref, k_ref, v_ref, o_ref, acc_ref, c_ref, *, tiles, pad):
    tq, tk = Q_TILE, KEY_BLOCK
    step = pl.program_id(2)

    lane = lax.broadcasted_iota(jnp.int32, (tq, LANES), 1)
    head0 = lane < SB_HEAD_DIM
    lane_k = lax.broadcasted_iota(jnp.int32, (tk, LANES), 1)
    head0_k = lane_k < SB_HEAD_DIM
    kr = lax.broadcasted_iota(jnp.int32, (tk, tk), 0)
    kc = lax.broadcasted_iota(jnp.int32, (tk, tk), 1)
    tri = jnp.where(kr > kc, -1.0, 0.0).astype(BF16)
    qrow = lax.broadcasted_iota(jnp.int32, (2 * tq, tk), 0)
    qrow = jnp.where(qrow >= tq, qrow - tq, qrow)
    kcol = lax.broadcasted_iota(jnp.int32, (2 * tq, tk), 1)
    causal = kcol < qrow + (tk - tq)

    def block(qs, r0, masked):
        k = k_ref[pl.ds(r0, tk), :]
        v = v_ref[pl.ds(r0, tk), :]
        z = lax.dot_general(qs, k, (((1,), (1,)), ((), ())), preferred_element_type=F32)
        sp = jnp.maximum(z, 0.0) + jnp.log(1.0 + jnp.exp2(jnp.abs(z) * -LOG2E))
        log_beta = z - sp
        if masked:
            sp = jnp.where(causal, sp, 0.0)
        hi = sp.astype(BF16)
        lo = (sp - hi.astype(F32)).astype(BF16)
        sfx = _dot(hi, tri) + _dot(lo, tri)
        w = jnp.exp(log_beta + sfx)
        if masked:
            w = jnp.where(causal, w, 0.0)
        w = w.astype(BF16)
        zv = jnp.zeros_like(v)
        pv0 = _dot(w[:tq], jnp.where(head0_k, v, zv))
        pv1 = _dot(w[tq:], jnp.where(head0_k, zv, v))
        total = sfx[:, 0:1] - sp[:, 0:1]
        return pv0, pv1, total

    def stacked_q(g):
        q = q_ref[g * tq:(g + 1) * tq, :]
        zq = jnp.zeros_like(q)
        return jnp.concatenate([jnp.where(head0, q, zq), jnp.where(head0, zq, q)], axis=0)

    ends = []
    for g in range(tiles):
        end = pl.multiple_of((step * tiles + g + 1) * tq + pad, tq)
        pv0, pv1, total = block(stacked_q(g), end - tk, True)
        acc_ref[g] = pv0 + pv1
        c_ref[g] = total
        ends.append((end - tk, jnp.max(total)))

    def more(carry):
        end, cmax = carry
        return (end > pad) & (cmax > EXIT_LOG)

    for g in range(tiles):
        def body(carry, g=g):
            end, _ = carry
            c = c_ref[g]
            pv0, pv1, total = block(stacked_q(g), pl.multiple_of(end - tk, tq), False)
            ec = jnp.exp(c)
            acc_ref[g] += ec[:tq] * pv0 + ec[tq:] * pv1
            c = c + total
            c_ref[g] = c
            return end - tk, jnp.max(c)

        lax.while_loop(more, body, ends[g])

    for g in range(tiles):
        o_ref[g * tq:(g + 1) * tq, :] = acc_ref[g].astype(o_ref.dtype)


def _sb_attention(qkv, *, d, pad, tiles):
    bsz, s_pad, _ = qkv.shape
    s = s_pad - pad
    pairs = d // LANES
    rows = tiles * Q_TILE
    return pl.pallas_call(
        functools.partial(_sb_attn_kernel, tiles=tiles, pad=pad),
        grid=(bsz, pairs, s // rows),
        in_specs=[
            pl.BlockSpec((None, rows, LANES), lambda b, p, i: (b, i + pad // rows, p)),
            pl.BlockSpec((None, s_pad, LANES), lambda b, p, i: (b, 0, pairs + p)),
            pl.BlockSpec((None, s_pad, LANES), lambda b, p, i: (b, 0, 2 * pairs + p)),
        ],
        out_specs=pl.BlockSpec((None, rows, LANES), lambda b, p, i: (b, i, p)),
        out_shape=jax.ShapeDtypeStruct((bsz, s, d), BF16),
        scratch_shapes=[
            pltpu.VMEM((tiles, Q_TILE, LANES), F32),
            pltpu.VMEM((tiles, 2 * Q_TILE, 1), F32),
        ],
        compiler_params=pltpu.CompilerParams(
            dimension_semantics=("arbitrary", "arbitrary", "arbitrary"),
            vmem_limit_bytes=VMEM_LIMIT),
        name="sb_attention",
    )(qkv, qkv, qkv)


def kernel(x, norm_mix_g, norm_ffn_g, a_w_in, a_conv_w, a_conv_b, a_w_r, a_b_r, a_w_i, a_b_i,
           a_lambda, a_w_out, b_w_qkv, b_w_out, ffn_w_gate, ffn_w_up, ffn_w_down, final_g):
    bsz, s, d = x.shape
    m = bsz * s
    row = lambda v: v.reshape(1, -1)
    bf = lambda w: w.astype(BF16)

    x = _rglru_mixer(x, row(norm_mix_g[0]), bf(a_w_in[0]), a_conv_w[0], row(a_conv_b[0]),
                     bf(a_w_r[0]), row(a_b_r[0]), bf(a_w_i[0]), row(a_b_i[0]),
                     row(a_lambda[0]), bf(a_w_out[0]), tm=256)
    x = _ffn(x.reshape(m, d), row(norm_ffn_g[0]), bf(ffn_w_gate[0]), bf(ffn_w_up[0]),
             bf(ffn_w_down[0]), tm=512, tf=256)

    pad = 512
    qkv = _qkv_proj(x.reshape(bsz, s, d), row(norm_mix_g[1]), bf(b_w_qkv[0]), tm=pad)
    att = _sb_attention(qkv, d=d, pad=pad, tiles=4)
    out = _ffn(x, row(norm_ffn_g[1]), bf(ffn_w_gate[1]), bf(ffn_w_up[1]), bf(ffn_w_down[1]),
               att=att.reshape(m, d), w_o=bf(b_w_out[0]), final_g=row(final_g), tm=512, tf=256)
    return out.reshape(bsz, s, d)
```

```python
import functools
import math

import jax
import jax.numpy as jnp
from jax import lax
from jax.experimental import pallas as pl
from jax.experimental.pallas import tpu as pltpu

RMS_EPS = 1e-6
RG_C = 8.0
CONV_W = 4
RG_BLOCKS = 4
SB_HEAD_DIM = 64
LANES = 128
SUBLANES = 8
Q_TILE = 128
KEY_BLOCK = 256
LOG2E = 1.4426950408889634
EXIT_LOG = -130.0
MASKED = -1e30
VMEM_LIMIT = 48 * 1024 * 1024

BF16 = jnp.bfloat16
F32 = jnp.float32


def _rmsnorm(x, g):
    ms = jnp.mean(x * x, axis=-1, keepdims=True)
    return x * lax.rsqrt(ms + RMS_EPS) * g


def _dot(a, b):
    return jnp.dot(a, b, preferred_element_type=F32)


def _softplus(z):
    return jnp.maximum(z, 0.0) + jnp.log(1.0 + jnp.exp(-jnp.abs(z)))


def _rglru_kernel(x_ref, g_ref, win_ref, cw_ref, cb_ref, wr_ref, br_ref, wi_ref, bi_ref,
                  lam_ref, wout_ref, o_ref, xpad_ref, a_ref, u_ref, hs_ref, h_ref):
    tm, d = x_ref.shape
    bw = d // RG_BLOCKS
    pad = SUBLANES

    @pl.when(pl.program_id(1) == 0)
    def _():
        xpad_ref[0:pad, :] = jnp.zeros((pad, d), F32)
        h_ref[...] = jnp.zeros_like(h_ref)

    x = x_ref[...]
    hb = _rmsnorm(x, g_ref[...]).astype(BF16)
    proj = _dot(hb, win_ref[...])
    gate = jax.nn.gelu(proj[:, :d], approximate=True)
    xpad_ref[pad:pad + tm, :] = proj[:, d:]

    xc = cb_ref[...] + cw_ref[CONV_W - 1:CONV_W, :] * xpad_ref[pad:pad + tm, :]
    for k in range(CONV_W - 1):
        off = pad - (CONV_W - 1) + k
        xc = xc + cw_ref[k:k + 1, :] * xpad_ref[off:off + tm, :]
    xpad_ref[0:pad, :] = xpad_ref[tm:tm + pad, :]

    xcb = xc.astype(BF16)
    lam = lam_ref[...]
    log_sig_lam = -_softplus(-lam)
    for n in range(RG_BLOCKS):
        sl = slice(n * bw, (n + 1) * bw)
        xb = xcb[:, sl]
        r = jax.nn.sigmoid(_dot(xb, wr_ref[n]) + br_ref[:, sl])
        i = jax.nn.sigmoid(_dot(xb, wi_ref[n]) + bi_ref[:, sl])
        log_a = RG_C * r * log_sig_lam[:, sl]
        a = jnp.exp(log_a)
        mult = jnp.sqrt(jnp.maximum(-jnp.tanh(log_a) * (a * a + 1.0), 0.0))
        a_ref[:, sl] = a
        u_ref[:, sl] = mult * (i * xc[:, sl])

    row = lax.broadcasted_iota(jnp.int32, (SUBLANES, d), 0)

    def scan_group(gi, h):
        r0 = pl.multiple_of(gi * SUBLANES, SUBLANES)
        a = a_ref[pl.ds(r0, SUBLANES), :]
        u = u_ref[pl.ds(r0, SUBLANES), :]
        s = 1
        while s < SUBLANES:
            keep = row >= s
            a_sh = jnp.where(keep, pltpu.roll(a, s, 0), 1.0)
            u_sh = jnp.where(keep, pltpu.roll(u, s, 0), 0.0)
            u = a * u_sh + u
            a = a * a_sh
            s *= 2
        hs = a * h + u
        hs_ref[pl.ds(r0, SUBLANES), :] = hs
        return hs[SUBLANES - 1:SUBLANES, :]

    h_last = lax.fori_loop(0, tm // SUBLANES, scan_group, h_ref[...])
    h_ref[...] = h_last

    y = (hs_ref[...] * gate).astype(BF16)
    o_ref[...] = x + _dot(y, wout_ref[...])


def _rglru_mixer(x, g, w_in, conv_w, conv_b, w_r, b_r, w_i, b_i, lam, w_out, *, tm):
    bsz, s, d = x.shape
    bw = d // RG_BLOCKS
    const2 = lambda b, i: (0, 0)
    const3 = lambda b, i: (0, 0, 0)
    vec = pl.BlockSpec((1, d), const2)
    return pl.pallas_call(
        _rglru_kernel,
        grid=(bsz, s // tm),
        in_specs=[
            pl.BlockSpec((None, tm, d), lambda b, i: (b, i, 0)),
            vec,
            pl.BlockSpec((d, 2 * d), const2),
            pl.BlockSpec((CONV_W, d), const2),
            vec,
            pl.BlockSpec((RG_BLOCKS, bw, bw), const3),
            vec,
            pl.BlockSpec((RG_BLOCKS, bw, bw), const3),
            vec,
            vec,
            pl.BlockSpec((d, d), const2),
        ],
        out_specs=pl.BlockSpec((None, tm, d), lambda b, i: (b, i, 0)),
        out_shape=jax.ShapeDtypeStruct((bsz, s, d), F32),
        scratch_shapes=[
            pltpu.VMEM((tm + SUBLANES, d), F32),
            pltpu.VMEM((tm, d), F32),
            pltpu.VMEM((tm, d), F32),
            pltpu.VMEM((tm, d), F32),
            pltpu.VMEM((1, d), F32),
        ],
        compiler_params=pltpu.CompilerParams(
            dimension_semantics=("arbitrary", "arbitrary"),
            vmem_limit_bytes=VMEM_LIMIT),
        name="rglru_mixer",
    )(x, g, w_in, conv_w, conv_b, w_r, b_r, w_i, b_i, lam, w_out)


def _ffn_kernel(*refs, pre_proj, final_norm):
    refs = list(refs)
    x_ref = refs.pop(0)
    if pre_proj:
        att_ref = refs.pop(0)
        wo_ref = refs.pop(0)
    g_ref, wg_ref, wu_ref, wd_ref = refs[:4]
    refs = refs[4:]
    if final_norm:
        fg_ref = refs.pop(0)
    o_ref, hb_ref = refs

    j = pl.program_id(1)

    @pl.when(j == 0)
    def _():
        x = x_ref[...]
        if pre_proj:
            x = x + _dot(att_ref[...], wo_ref[...])
        o_ref[...] = x
        hb_ref[...] = _rmsnorm(x, g_ref[...]).astype(BF16)

    h = hb_ref[...]
    gate = _dot(h, wg_ref[...])
    up = _dot(h, wu_ref[...])
    act = (gate * jax.nn.sigmoid(gate) * up).astype(BF16)
    o_ref[...] += _dot(act, wd_ref[...])

    if final_norm:
        @pl.when(j == pl.num_programs(1) - 1)
        def _():
            o_ref[...] = _rmsnorm(o_ref[...], fg_ref[...])


def _ffn(x, g, w_gate, w_up, w_down, *, att=None, w_o=None, final_g=None, tm, tf):
    m, d = x.shape
    dff = w_gate.shape[1]
    pre_proj = att is not None
    final_norm = final_g is not None
    row = pl.BlockSpec((tm, d), lambda i, j: (i, 0))
    vec = pl.BlockSpec((1, d), lambda i, j: (0, 0))
    args, specs = [x], [row]
    if pre_proj:
        args += [att, w_o]
        specs += [row, pl.BlockSpec((d, d), lambda i, j: (0, 0))]
    args += [g, w_gate, w_up, w_down]
    specs += [vec,
              pl.BlockSpec((d, tf), lambda i, j: (0, j)),
              pl.BlockSpec((d, tf), lambda i, j: (0, j)),
              pl.BlockSpec((tf, d), lambda i, j: (j, 0))]
    if final_norm:
        args.append(final_g)
        specs.append(vec)
    return pl.pallas_call(
        functools.partial(_ffn_kernel, pre_proj=pre_proj, final_norm=final_norm),
        grid=(m // tm, dff // tf),
        in_specs=specs,
        out_specs=row,
        out_shape=jax.ShapeDtypeStruct((m, d), F32),
        scratch_shapes=[pltpu.VMEM((tm, d), BF16)],
        compiler_params=pltpu.CompilerParams(
            dimension_semantics=("arbitrary", "arbitrary"),
            vmem_limit_bytes=VMEM_LIMIT),
        name="ffn_attn_out" if pre_proj else "ffn",
    )(*args)


def _qkv_kernel(x_ref, g_ref, w_ref, o_ref, hb_ref, *, q_scale):
    i = pl.program_id(1)
    j = pl.program_id(2)

    @pl.when(i == 0)
    def _():
        o_ref[...] = jnp.zeros_like(o_ref)

    @pl.when((i > 0) & (j == 0))
    def _():
        hb_ref[...] = _rmsnorm(x_ref[...], g_ref[...]).astype(BF16)

    @pl.when(i > 0)
    def _():
        scale = jnp.where(j == 0, q_scale, 1.0).astype(F32)
        o_ref[...] = (_dot(hb_ref[...], w_ref[...]) * scale).astype(o_ref.dtype)


def _qkv_proj(x, g, w_qkv, *, tm):
    bsz, s, d = x.shape
    n = w_qkv.shape[1]
    return pl.pallas_call(
        functools.partial(_qkv_kernel, q_scale=LOG2E / math.sqrt(SB_HEAD_DIM)),
        grid=(bsz, s // tm + 1, n // d),
        in_specs=[
            pl.BlockSpec((None, tm, d), lambda b, i, j: (b, jnp.maximum(i - 1, 0), 0)),
            pl.BlockSpec((1, d), lambda b, i, j: (0, 0)),
            pl.BlockSpec((d, d), lambda b, i, j: (0, j)),
        ],
        out_specs=pl.BlockSpec((None, tm, d), lambda b, i, j: (b, i, j)),
        out_shape=jax.ShapeDtypeStruct((bsz, s + tm, n), BF16),
        scratch_shapes=[pltpu.VMEM((tm, d), BF16)],
        compiler_params=pltpu.CompilerParams(
            dimension_semantics=("arbitrary", "arbitrary", "arbitrary"),
            vmem_limit_bytes=VMEM_LIMIT),
        name="qkv_proj",
    )(x, g, w_qkv)


def _sb_attn_kernel(q_ref, k_ref, v_ref, tri_ref, bias_ref, o_ref, acc_ref, c_ref, *, tiles, pad):
    tq, tk = Q_TILE, KEY_BLOCK
    step = pl.program_id(2)

    lane = lax.broadcasted_iota(jnp.int32, (tq, LANES), 1)
    head0, head1 = lane < SB_HEAD_DIM, lane >= SB_HEAD_DIM

    def scores(g, r0, first):
        q = q_ref[g * tq:(g + 1) * tq, :]
        zq = jnp.zeros_like(q)
        qs = jnp.concatenate([jnp.where(head0, q, zq), jnp.where(head1, q, zq)], axis=0)
        k = k_ref[pl.ds(r0, tk), :]
        z = lax.dot_general(qs, k, (((1,), (1,)), ((), ())), preferred_element_type=F32)
        if first:
            z = jnp.concatenate([z[:, :tq], z[:, tq:] + bias_ref[...]], axis=1)
        return z

    def weights(z):
        zneg, zpos = jnp.minimum(z, 0.0), jnp.maximum(z, 0.0)
        lg = jnp.log(1.0 + jnp.exp2(zneg - zpos)) * LOG2E
        sp = zpos + lg
        sfx = _dot(sp.astype(BF16), tri_ref[...])
        w = jnp.exp2((zneg - lg) + sfx)
        total = sfx[:, 0:1] - sp[:, 0:1]
        return w.astype(BF16), total

    def weighted_values(w, r0):
        return _dot(w, v_ref[pl.ds(r0, tk), :])

    rows2 = 2 * tq
    ends = [pl.multiple_of((step * tiles + g + 1) * tq + pad, tq) for g in range(tiles)]
    w, total = weights(jnp.concatenate(
        [scores(g, ends[g] - tk, True) for g in range(tiles)], axis=0))
    cmax = []
    for g in range(tiles):
        pv = weighted_values(w[g * rows2:(g + 1) * rows2], ends[g] - tk)
        acc_ref[g] = jnp.where(head0, pv[:tq], pv[tq:])
        c = total[g * rows2:(g + 1) * rows2]
        c_ref[g] = c
        cmax.append(jnp.max(c))

    def more(carry):
        end, cm = carry
        return (end > pad) & (cm > EXIT_LOG)

    for g in range(tiles):
        def body(carry, g=g):
            end, _ = carry
            r0 = pl.multiple_of(end - tk, tq)
            w, total = weights(scores(g, r0, False))
            c = c_ref[g]
            pv = jnp.exp2(c) * weighted_values(w, r0)
            acc_ref[g] += jnp.where(head0, pv[:tq], pv[tq:])
            c = c + total
            c_ref[g] = c
            return end - tk, jnp.max(c)

        lax.while_loop(more, body, (ends[g] - tk, cmax[g]))

    for g in range(tiles):
        o_ref[g * tq:(g + 1) * tq, :] = acc_ref[g].astype(o_ref.dtype)


def _sb_attention(qkv, *, d, pad, tiles):
    bsz, s_pad, _ = qkv.shape
    s = s_pad - pad
    pairs = d // LANES
    rows = tiles * Q_TILE
    assert pad % rows == 0 and s % rows == 0
    tq, tk = Q_TILE, KEY_BLOCK
    kr = lax.broadcasted_iota(jnp.int32, (tk, tk), 0)
    kc = lax.broadcasted_iota(jnp.int32, (tk, tk), 1)
    tri = jnp.where(kr > kc, -1.0, 0.0).astype(BF16)
    mr = lax.broadcasted_iota(jnp.int32, (2 * tq, tk - tq), 0) & (tq - 1)
    mc = lax.broadcasted_iota(jnp.int32, (2 * tq, tk - tq), 1)
    bias = jnp.where(mc < mr, 0.0, MASKED).astype(F32)
    const = lambda b, p, i: (0, 0)
    return pl.pallas_call(
        functools.partial(_sb_attn_kernel, tiles=tiles, pad=pad),
        grid=(bsz, pairs, s // rows),
        in_specs=[
            pl.BlockSpec((None, rows, LANES), lambda b, p, i: (b, i + pad // rows, p)),
            pl.BlockSpec((None, s_pad, LANES), lambda b, p, i: (b, 0, pairs + p)),
            pl.BlockSpec((None, s_pad, LANES), lambda b, p, i: (b, 0, 2 * pairs + p)),
            pl.BlockSpec((tk, tk), const),
            pl.BlockSpec((2 * tq, tk - tq), const),
        ],
        out_specs=pl.BlockSpec((None, rows, LANES), lambda b, p, i: (b, i, p)),
        out_shape=jax.ShapeDtypeStruct((bsz, s, d), BF16),
        scratch_shapes=[
            pltpu.VMEM((tiles, Q_TILE, LANES), F32),
            pltpu.VMEM((tiles, 2 * Q_TILE, 1), F32),
        ],
        compiler_params=pltpu.CompilerParams(
            dimension_semantics=("arbitrary", "arbitrary", "arbitrary"),
            vmem_limit_bytes=VMEM_LIMIT),
        name="sb_attention",
    )(qkv, qkv, qkv, tri, bias)


def kernel(x, norm_mix_g, norm_ffn_g, a_w_in, a_conv_w, a_conv_b, a_w_r, a_b_r, a_w_i, a_b_i,
           a_lambda, a_w_out, b_w_qkv, b_w_out, ffn_w_gate, ffn_w_up, ffn_w_down, final_g):
    bsz, s, d = x.shape
    m = bsz * s
    row = lambda v: v.reshape(1, -1)
    bf = lambda w: w.astype(BF16)

    x = _rglru_mixer(x, row(norm_mix_g[0]), bf(a_w_in[0]), a_conv_w[0], row(a_conv_b[0]),
                     bf(a_w_r[0]), row(a_b_r[0]), bf(a_w_i[0]), row(a_b_i[0]),
                     row(a_lambda[0]), bf(a_w_out[0]), tm=256)
    x = _ffn(x.reshape(m, d), row(norm_ffn_g[0]), bf(ffn_w_gate[0]), bf(ffn_w_up[0]),
             bf(ffn_w_down[0]), tm=1024, tf=256)

    tiles = 8
    pad = tiles * Q_TILE
    qkv = _qkv_proj(x.reshape(bsz, s, d), row(norm_mix_g[1]), bf(b_w_qkv[0]), tm=pad)
    att = _sb_attention(qkv, d=d, pad=pad, tiles=tiles)
    out = _ffn(x, row(norm_ffn_g[1]), bf(ffn_w_gate[1]), bf(ffn_w_up[1]), bf(ffn_w_down[1]),
               att=att.reshape(m, d), w_o=bf(b_w_out[0]), final_g=row(final_g), tm=1024, tf=256)
    return out.reshape(bsz, s, d)
```

```python
import functools
import math

import jax
import jax.numpy as jnp
from jax import lax
from jax.experimental import pallas as pl
from jax.experimental.pallas import tpu as pltpu

RMS_EPS = 1e-6
RG_C = 8.0
CONV_W = 4
RG_BLOCKS = 4
SB_HEAD_DIM = 64
LANES = 128
SUBLANES = 8
Q_TILE = 64
KEY_BLOCK = 256
LOG2E = 1.4426950408889634
EXIT_LOG = -130.0
MASKED = -1e30
VMEM_LIMIT = 48 * 1024 * 1024

BF16 = jnp.bfloat16
F32 = jnp.float32


def _rmsnorm(x, g):
    ms = jnp.mean(x * x, axis=-1, keepdims=True)
    return x * lax.rsqrt(ms + RMS_EPS) * g


def _dot(a, b):
    return jnp.dot(a, b, preferred_element_type=F32)


def _softplus(z):
    return jnp.maximum(z, 0.0) + jnp.log(1.0 + jnp.exp(-jnp.abs(z)))


def _rglru_kernel(x_ref, g_ref, win_ref, cw_ref, cb_ref, wr_ref, br_ref, wi_ref, bi_ref,
                  lam_ref, wout_ref, o_ref, xpad_ref, a_ref, u_ref, hs_ref, h_ref):
    tm, d = x_ref.shape
    bw = d // RG_BLOCKS
    pad = SUBLANES

    @pl.when(pl.program_id(1) == 0)
    def _():
        xpad_ref[0:pad, :] = jnp.zeros((pad, d), F32)
        h_ref[...] = jnp.zeros_like(h_ref)

    x = x_ref[...]
    hb = _rmsnorm(x, g_ref[...]).astype(BF16)
    proj = _dot(hb, win_ref[...])
    gate = jax.nn.gelu(proj[:, :d], approximate=True)
    xpad_ref[pad:pad + tm, :] = proj[:, d:]

    xc = cb_ref[...] + cw_ref[CONV_W - 1:CONV_W, :] * xpad_ref[pad:pad + tm, :]
    for k in range(CONV_W - 1):
        off = pad - (CONV_W - 1) + k
        xc = xc + cw_ref[k:k + 1, :] * xpad_ref[off:off + tm, :]
    xpad_ref[0:pad, :] = xpad_ref[tm:tm + pad, :]

    xcb = xc.astype(BF16)
    lam = lam_ref[...]
    log_sig_lam = -_softplus(-lam)
    for n in range(RG_BLOCKS):
        sl = slice(n * bw, (n + 1) * bw)
        xb = xcb[:, sl]
        r = jax.nn.sigmoid(_dot(xb, wr_ref[n]) + br_ref[:, sl])
        i = jax.nn.sigmoid(_dot(xb, wi_ref[n]) + bi_ref[:, sl])
        log_a = RG_C * r * log_sig_lam[:, sl]
        a = jnp.exp(log_a)
        mult = jnp.sqrt(jnp.maximum(-jnp.tanh(log_a) * (a * a + 1.0), 0.0))
        a_ref[:, sl] = a
        u_ref[:, sl] = mult * (i * xc[:, sl])

    row = lax.broadcasted_iota(jnp.int32, (SUBLANES, d), 0)

    def scan_group(gi, h):
        r0 = pl.multiple_of(gi * SUBLANES, SUBLANES)
        a = a_ref[pl.ds(r0, SUBLANES), :]
        u = u_ref[pl.ds(r0, SUBLANES), :]
        s = 1
        while s < SUBLANES:
            keep = row >= s
            a_sh = jnp.where(keep, pltpu.roll(a, s, 0), 1.0)
            u_sh = jnp.where(keep, pltpu.roll(u, s, 0), 0.0)
            u = a * u_sh + u
            a = a * a_sh
            s *= 2
        hs = a * h + u
        hs_ref[pl.ds(r0, SUBLANES), :] = hs
        return hs[SUBLANES - 1:SUBLANES, :]

    h_last = lax.fori_loop(0, tm // SUBLANES, scan_group, h_ref[...])
    h_ref[...] = h_last

    y = (hs_ref[...] * gate).astype(BF16)
    o_ref[...] = x + _dot(y, wout_ref[...])


def _rglru_mixer(x, g, w_in, conv_w, conv_b, w_r, b_r, w_i, b_i, lam, w_out, *, tm):
    bsz, s, d = x.shape
    bw = d // RG_BLOCKS
    const2 = lambda b, i: (0, 0)
    const3 = lambda b, i: (0, 0, 0)
    vec = pl.BlockSpec((1, d), const2)
    return pl.pallas_call(
        _rglru_kernel,
        grid=(bsz, s // tm),
        in_specs=[
            pl.BlockSpec((None, tm, d), lambda b, i: (b, i, 0)),
            vec,
            pl.BlockSpec((d, 2 * d), const2),
            pl.BlockSpec((CONV_W, d), const2),
            vec,
            pl.BlockSpec((RG_BLOCKS, bw, bw), const3),
            vec,
            pl.BlockSpec((RG_BLOCKS, bw, bw), const3),
            vec,
            vec,
            pl.BlockSpec((d, d), const2),
        ],
        out_specs=pl.BlockSpec((None, tm, d), lambda b, i: (b, i, 0)),
        out_shape=jax.ShapeDtypeStruct((bsz, s, d), F32),
        scratch_shapes=[
            pltpu.VMEM((tm + SUBLANES, d), F32),
            pltpu.VMEM((tm, d), F32),
            pltpu.VMEM((tm, d), F32),
            pltpu.VMEM((tm, d), F32),
            pltpu.VMEM((1, d), F32),
        ],
        compiler_params=pltpu.CompilerParams(
            dimension_semantics=("arbitrary", "arbitrary"),
            vmem_limit_bytes=VMEM_LIMIT),
        name="rglru_mixer",
    )(x, g, w_in, conv_w, conv_b, w_r, b_r, w_i, b_i, lam, w_out)


def _ffn_kernel(*refs, pre_proj, final_norm):
    refs = list(refs)
    x_ref = refs.pop(0)
    if pre_proj:
        att_ref = refs.pop(0)
        wo_ref = refs.pop(0)
    g_ref, wg_ref, wu_ref, wd_ref = refs[:4]
    refs = refs[4:]
    if final_norm:
        fg_ref = refs.pop(0)
    o_ref, hb_ref = refs

    j = pl.program_id(1)

    @pl.when(j == 0)
    def _():
        x = x_ref[...]
        if pre_proj:
            x = x + _dot(att_ref[...], wo_ref[...])
        o_ref[...] = x
        hb_ref[...] = _rmsnorm(x, g_ref[...]).astype(BF16)

    h = hb_ref[...]
    gate = _dot(h, wg_ref[...])
    up = _dot(h, wu_ref[...])
    act = (gate * jax.nn.sigmoid(gate) * up).astype(BF16)
    o_ref[...] += _dot(act, wd_ref[...])

    if final_norm:
        @pl.when(j == pl.num_programs(1) - 1)
        def _():
            o_ref[...] = _rmsnorm(o_ref[...], fg_ref[...])


def _ffn(x, g, w_gate, w_up, w_down, *, att=None, w_o=None, final_g=None, tm, tf):
    m, d = x.shape
    dff = w_gate.shape[1]
    pre_proj = att is not None
    final_norm = final_g is not None
    row = pl.BlockSpec((tm, d), lambda i, j: (i, 0))
    vec = pl.BlockSpec((1, d), lambda i, j: (0, 0))
    args, specs = [x], [row]
    if pre_proj:
        args += [att, w_o]
        specs += [row, pl.BlockSpec((d, d), lambda i, j: (0, 0))]
    args += [g, w_gate, w_up, w_down]
    specs += [vec,
              pl.BlockSpec((d, tf), lambda i, j: (0, j)),
              pl.BlockSpec((d, tf), lambda i, j: (0, j)),
              pl.BlockSpec((tf, d), lambda i, j: (j, 0))]
    if final_norm:
        args.append(final_g)
        specs.append(vec)
    return pl.pallas_call(
        functools.partial(_ffn_kernel, pre_proj=pre_proj, final_norm=final_norm),
        grid=(m // tm, dff // tf),
        in_specs=specs,
        out_specs=row,
        out_shape=jax.ShapeDtypeStruct((m, d), F32),
        scratch_shapes=[pltpu.VMEM((tm, d), BF16)],
        compiler_params=pltpu.CompilerParams(
            dimension_semantics=("arbitrary", "arbitrary"),
            vmem_limit_bytes=VMEM_LIMIT),
        name="ffn_attn_out" if pre_proj else "ffn",
    )(*args)


def _qkv_kernel(x_ref, g_ref, w_ref, o_ref, hb_ref, *, q_scale):
    i = pl.program_id(1)
    j = pl.program_id(2)

    @pl.when(i == 0)
    def _():
        o_ref[...] = jnp.zeros_like(o_ref)

    @pl.when((i > 0) & (j == 0))
    def _():
        hb_ref[...] = _rmsnorm(x_ref[...], g_ref[...]).astype(BF16)

    @pl.when(i > 0)
    def _():
        scale = jnp.where(j == 0, q_scale, 1.0).astype(F32)
        o_ref[...] = (_dot(hb_ref[...], w_ref[...]) * scale).astype(o_ref.dtype)


def _qkv_proj(x, g, w_qkv, *, tm):
    bsz, s, d = x.shape
    n = w_qkv.shape[1]
    return pl.pallas_call(
        functools.partial(_qkv_kernel, q_scale=LOG2E / math.sqrt(SB_HEAD_DIM)),
        grid=(bsz, s // tm + 1, n // d),
        in_specs=[
            pl.BlockSpec((None, tm, d), lambda b, i, j: (b, jnp.maximum(i - 1, 0), 0)),
            pl.BlockSpec((1, d), lambda b, i, j: (0, 0)),
            pl.BlockSpec((d, d), lambda b, i, j: (0, j)),
        ],
        out_specs=pl.BlockSpec((None, tm, d), lambda b, i, j: (b, i, j)),
        out_shape=jax.ShapeDtypeStruct((bsz, s + tm, n), BF16),
        scratch_shapes=[pltpu.VMEM((tm, d), BF16)],
        compiler_params=pltpu.CompilerParams(
            dimension_semantics=("arbitrary", "arbitrary", "arbitrary"),
            vmem_limit_bytes=VMEM_LIMIT),
        name="qkv_proj",
    )(x, g, w_qkv)


def _sb_attn_kernel(q_ref, k_ref, v_ref, tri_ref, bias_ref, o_ref, acc_ref, c_ref, cmax_ref, *,
                    tiles, pad):
    tq, tk = Q_TILE, KEY_BLOCK
    step = pl.program_id(2)

    lane = lax.broadcasted_iota(jnp.int32, (tq, LANES), 1)
    head0, head1 = lane < SB_HEAD_DIM, lane >= SB_HEAD_DIM

    def scores(q, r0, first):
        zq = jnp.zeros_like(q)
        qs = jnp.concatenate([jnp.where(head0, q, zq), jnp.where(head1, q, zq)], axis=0)
        k = k_ref[pl.ds(r0, tk), :]
        z = lax.dot_general(qs, k, (((1,), (1,)), ((), ())), preferred_element_type=F32)
        if first:
            z = jnp.concatenate([z[:, :tk - LANES], z[:, tk - LANES:] + bias_ref[...]], axis=1)
        return z

    def weights(z):
        zneg, zpos = jnp.minimum(z, 0.0), jnp.maximum(z, 0.0)
        lg = jnp.log(1.0 + jnp.exp2(zneg - zpos)) * LOG2E
        sp = zpos + lg
        sfx = _dot(sp.astype(BF16), tri_ref[...])
        w = jnp.exp2((zneg - lg) + sfx)
        total = sfx[:, 0:1] - sp[:, 0:1]
        return w.astype(BF16), total

    def weighted_values(w, r0):
        return _dot(w, v_ref[pl.ds(r0, tk), :])

    rows2 = 2 * tq
    ends = [pl.multiple_of((step * tiles + g + 1) * tq + pad, tq) for g in range(tiles)]
    w, total = weights(jnp.concatenate(
        [scores(q_ref[g * tq:(g + 1) * tq, :], ends[g] - tk, True) for g in range(tiles)],
        axis=0))
    for g in range(tiles):
        pv = weighted_values(w[g * rows2:(g + 1) * rows2], ends[g] - tk)
        acc_ref[g] = jnp.where(head0, pv[:tq], pv[tq:])
        c = total[g * rows2:(g + 1) * rows2]
        c_ref[g] = c
        cmax_ref[g] = jnp.max(c)

    def more(carry):
        end, cm = carry
        return (end > pad) & (cm > EXIT_LOG)

    def walk(g, _):
        q = q_ref[pl.ds(pl.multiple_of(g * tq, tq), tq), :]

        def body(carry):
            end, _ = carry
            r0 = pl.multiple_of(end - tk, tq)
            w, total = weights(scores(q, r0, False))
            c = c_ref[g]
            pv = jnp.exp2(c) * weighted_values(w, r0)
            acc_ref[g] += jnp.where(head0, pv[:tq], pv[tq:])
            c = c + total
            c_ref[g] = c
            return end - tk, jnp.max(c)

        first_start = (step * tiles + g + 1) * tq + pad - tk
        lax.while_loop(more, body, (first_start, cmax_ref[g]))
        return 0

    lax.fori_loop(0, tiles, walk, 0)

    for g in range(tiles):
        o_ref[g * tq:(g + 1) * tq, :] = acc_ref[g].astype(o_ref.dtype)


def _sb_attention(qkv, *, d, pad, tiles):
    bsz, s_pad, _ = qkv.shape
    s = s_pad - pad
    pairs = d // LANES
    rows = tiles * Q_TILE
    assert pad % rows == 0 and s % rows == 0
    tq, tk = Q_TILE, KEY_BLOCK
    kr = lax.broadcasted_iota(jnp.int32, (tk, tk), 0)
    kc = lax.broadcasted_iota(jnp.int32, (tk, tk), 1)
    tri = jnp.where(kr > kc, -1.0, 0.0).astype(BF16)
    assert tq <= LANES and tk % LANES == 0
    mr = lax.broadcasted_iota(jnp.int32, (2 * tq, LANES), 0) & (tq - 1)
    mc = lax.broadcasted_iota(jnp.int32, (2 * tq, LANES), 1)
    bias = jnp.where(mc - (LANES - tq) < mr, 0.0, MASKED).astype(F32)
    const = lambda b, p, i: (0, 0)
    return pl.pallas_call(
        functools.partial(_sb_attn_kernel, tiles=tiles, pad=pad),
        grid=(bsz, pairs, s // rows),
        in_specs=[
            pl.BlockSpec((None, rows, LANES), lambda b, p, i: (b, i + pad // rows, p)),
            pl.BlockSpec((None, s_pad, LANES), lambda b, p, i: (b, 0, pairs + p)),
            pl.BlockSpec((None, s_pad, LANES), lambda b, p, i: (b, 0, 2 * pairs + p)),
            pl.BlockSpec((tk, tk), const),
            pl.BlockSpec((2 * tq, LANES), const),
        ],
        out_specs=pl.BlockSpec((None, rows, LANES), lambda b, p, i: (b, i, p)),
        out_shape=jax.ShapeDtypeStruct((bsz, s, d), BF16),
        scratch_shapes=[
            pltpu.VMEM((tiles, Q_TILE, LANES), F32),
            pltpu.VMEM((tiles, 2 * Q_TILE, 1), F32),
            pltpu.SMEM((tiles,), F32),
        ],
        compiler_params=pltpu.CompilerParams(
            dimension_semantics=("arbitrary", "arbitrary", "arbitrary"),
            vmem_limit_bytes=VMEM_LIMIT),
        name="sb_attention",
    )(qkv, qkv, qkv, tri, bias)


def kernel(x, norm_mix_g, norm_ffn_g, a_w_in, a_conv_w, a_conv_b, a_w_r, a_b_r, a_w_i, a_b_i,
           a_lambda, a_w_out, b_w_qkv, b_w_out, ffn_w_gate, ffn_w_up, ffn_w_down, final_g):
    bsz, s, d = x.shape
    m = bsz * s
    row = lambda v: v.reshape(1, -1)
    bf = lambda w: w.astype(BF16)

    x = _rglru_mixer(x, row(norm_mix_g[0]), bf(a_w_in[0]), a_conv_w[0], row(a_conv_b[0]),
                     bf(a_w_r[0]), row(a_b_r[0]), bf(a_w_i[0]), row(a_b_i[0]),
                     row(a_lambda[0]), bf(a_w_out[0]), tm=256)
    x = _ffn(x.reshape(m, d), row(norm_ffn_g[0]), bf(ffn_w_gate[0]), bf(ffn_w_up[0]),
             bf(ffn_w_down[0]), tm=1024, tf=256)

    tiles = 16
    pad = tiles * Q_TILE
    qkv = _qkv_proj(x.reshape(bsz, s, d), row(norm_mix_g[1]), bf(b_w_qkv[0]), tm=pad)
    att = _sb_attention(qkv, d=d, pad=pad, tiles=tiles)
    out = _ffn(x, row(norm_ffn_g[1]), bf(ffn_w_gate[1]), bf(ffn_w_up[1]), bf(ffn_w_down[1]),
               att=att.reshape(m, d), w_o=bf(b_w_out[0]), final_g=row(final_g), tm=1024, tf=256)
    return out.reshape(bsz, s, d)
```

```python
import functools
import math

import jax
import jax.numpy as jnp
from jax import lax
from jax.experimental import pallas as pl
from jax.experimental.pallas import tpu as pltpu

RMS_EPS = 1e-6
RG_C = 8.0
CONV_W = 4
RG_BLOCKS = 4
SB_HEAD_DIM = 64
LANES = 128
SUBLANES = 8
Q_TILE = 64
KEY_BLOCK = 256
LOG2E = 1.4426950408889634
GELU_C0 = math.sqrt(2.0 / math.pi)
EXIT_LOG = -130.0
MASKED = -1e30
VMEM_LIMIT = 48 * 1024 * 1024

BF16 = jnp.bfloat16
F32 = jnp.float32


def _rmsnorm(x, g):
    ms = jnp.mean(x * x, axis=-1, keepdims=True)
    return x * lax.rsqrt(ms + RMS_EPS) * g


def _dot(a, b):
    return jnp.dot(a, b, preferred_element_type=F32)


def _gelu_tanh(x):
    inner = x * (GELU_C0 + (GELU_C0 * 0.044715) * (x * x))
    hx = 0.5 * x
    return hx + hx * jnp.tanh(inner)


def _softplus(z):
    return jnp.maximum(z, 0.0) + jnp.log(1.0 + jnp.exp(-jnp.abs(z)))


def _rglru_kernel(x_ref, g_ref, win_ref, cw_ref, cb_ref, wr_ref, br_ref, wi_ref, bi_ref,
                  lam_ref, wout_ref, o_ref, xpad_ref, a_ref, u_ref, hs_ref, h_ref):
    tm, d = x_ref.shape
    bw = d // RG_BLOCKS
    pad = SUBLANES

    @pl.when(pl.program_id(1) == 0)
    def _():
        xpad_ref[0:pad, :] = jnp.zeros((pad, d), F32)
        h_ref[...] = jnp.zeros_like(h_ref)

    x = x_ref[...]
    hb = _rmsnorm(x, g_ref[...]).astype(BF16)
    proj = _dot(hb, win_ref[...])
    gate = _gelu_tanh(proj[:, :d])
    xpad_ref[pad:pad + tm, :] = proj[:, d:]

    xc = cb_ref[...] + cw_ref[CONV_W - 1:CONV_W, :] * xpad_ref[pad:pad + tm, :]
    for k in range(CONV_W - 1):
        off = pad - (CONV_W - 1) + k
        xc = xc + cw_ref[k:k + 1, :] * xpad_ref[off:off + tm, :]
    xpad_ref[0:pad, :] = xpad_ref[tm:tm + pad, :]

    xcb = xc.astype(BF16)
    lam = lam_ref[...]
    log_sig_lam = -_softplus(-lam)
    for n in range(RG_BLOCKS):
        sl = slice(n * bw, (n + 1) * bw)
        xb = xcb[:, sl]
        r = jax.nn.sigmoid(_dot(xb, wr_ref[n]) + br_ref[:, sl])
        i = jax.nn.sigmoid(_dot(xb, wi_ref[n]) + bi_ref[:, sl])
        log_a = RG_C * r * log_sig_lam[:, sl]
        a = jnp.exp(log_a)
        mult = jnp.exp2(jnp.log(jnp.maximum(-jnp.tanh(log_a) * (a * a + 1.0), 0.0))
                        * (0.5 * LOG2E))
        a_ref[:, sl] = a
        u_ref[:, sl] = mult * (i * xc[:, sl])

    row = lax.broadcasted_iota(jnp.int32, (SUBLANES, d), 0)

    def scan_group(gi, h):
        r0 = pl.multiple_of(gi * SUBLANES, SUBLANES)
        a = a_ref[pl.ds(r0, SUBLANES), :]
        u = u_ref[pl.ds(r0, SUBLANES), :]
        s = 1
        while s < SUBLANES:
            keep = row >= s
            a_sh = jnp.where(keep, pltpu.roll(a, s, 0), 1.0)
            u_sh = jnp.where(keep, pltpu.roll(u, s, 0), 0.0)
            u = a * u_sh + u
            a = a * a_sh
            s *= 2
        hs = a * h + u
        hs_ref[pl.ds(r0, SUBLANES), :] = hs
        return hs[SUBLANES - 1:SUBLANES, :]

    h_last = lax.fori_loop(0, tm // SUBLANES, scan_group, h_ref[...])
    h_ref[...] = h_last

    y = (hs_ref[...] * gate).astype(BF16)
    o_ref[...] = x + _dot(y, wout_ref[...])


def _rglru_mixer(x, g, w_in, conv_w, conv_b, w_r, b_r, w_i, b_i, lam, w_out, *, tm):
    bsz, s, d = x.shape
    bw = d // RG_BLOCKS
    const2 = lambda b, i: (0, 0)
    const3 = lambda b, i: (0, 0, 0)
    vec = pl.BlockSpec((1, d), const2)
    return pl.pallas_call(
        _rglru_kernel,
        grid=(bsz, s // tm),
        in_specs=[
            pl.BlockSpec((None, tm, d), lambda b, i: (b, i, 0)),
            vec,
            pl.BlockSpec((d, 2 * d), const2),
            pl.BlockSpec((CONV_W, d), const2),
            vec,
            pl.BlockSpec((RG_BLOCKS, bw, bw), const3),
            vec,
            pl.BlockSpec((RG_BLOCKS, bw, bw), const3),
            vec,
            vec,
            pl.BlockSpec((d, d), const2),
        ],
        out_specs=pl.BlockSpec((None, tm, d), lambda b, i: (b, i, 0)),
        out_shape=jax.ShapeDtypeStruct((bsz, s, d), F32),
        scratch_shapes=[
            pltpu.VMEM((tm + SUBLANES, d), F32),
            pltpu.VMEM((tm, d), F32),
            pltpu.VMEM((tm, d), F32),
            pltpu.VMEM((tm, d), F32),
            pltpu.VMEM((1, d), F32),
        ],
        compiler_params=pltpu.CompilerParams(
            dimension_semantics=("arbitrary", "arbitrary"),
            vmem_limit_bytes=VMEM_LIMIT),
        name="rglru_mixer",
    )(x, g, w_in, conv_w, conv_b, w_r, b_r, w_i, b_i, lam, w_out)


def _ffn_kernel(*refs, pre_proj, final_norm):
    refs = list(refs)
    x_ref = refs.pop(0)
    if pre_proj:
        att_ref = refs.pop(0)
        wo_ref = refs.pop(0)
    g_ref, wg_ref, wu_ref, wd_ref = refs[:4]
    refs = refs[4:]
    if final_norm:
        fg_ref = refs.pop(0)
    (o_ref,) = refs

    x = x_ref[...]
    if pre_proj:
        x = x + _dot(att_ref[...], wo_ref[...])
    h = _rmsnorm(x, g_ref[...]).astype(BF16)
    o_ref[...] = x
    for c in range(wg_ref.shape[0]):
        gate = _dot(h, wg_ref[c])
        up = _dot(h, wu_ref[c])
        act = (gate * jax.nn.sigmoid(gate) * up).astype(BF16)
        o_ref[...] += _dot(act, wd_ref[c])
    if final_norm:
        o_ref[...] = _rmsnorm(o_ref[...], fg_ref[...])


def _ffn(x, g, w_gate, w_up, w_down, *, att=None, w_o=None, final_g=None, tm, tf):
    m, d = x.shape
    dff = w_gate.shape[1]
    nchunk = dff // tf
    pre_proj = att is not None
    final_norm = final_g is not None
    resident = dict(pipeline_mode=pl.Buffered(1))
    row = pl.BlockSpec((tm, d), lambda i: (i, 0))
    vec = pl.BlockSpec((1, d), lambda i: (0, 0), **resident)
    args, specs = [x], [row]
    if pre_proj:
        args += [att, w_o]
        specs += [row, pl.BlockSpec((d, d), lambda i: (0, 0), **resident)]
    to_slabs = lambda w: w.reshape(d, nchunk, tf).transpose(1, 0, 2)
    args += [g, to_slabs(w_gate), to_slabs(w_up), w_down.reshape(nchunk, tf, d)]
    specs += [vec,
              pl.BlockSpec((nchunk, d, tf), lambda i: (0, 0, 0), **resident),
              pl.BlockSpec((nchunk, d, tf), lambda i: (0, 0, 0), **resident),
              pl.BlockSpec((nchunk, tf, d), lambda i: (0, 0, 0), **resident)]
    if final_norm:
        args.append(final_g)
        specs.append(vec)
    return pl.pallas_call(
        functools.partial(_ffn_kernel, pre_proj=pre_proj, final_norm=final_norm),
        grid=(m // tm,),
        in_specs=specs,
        out_specs=row,
        out_shape=jax.ShapeDtypeStruct((m, d), F32),
        compiler_params=pltpu.CompilerParams(
            dimension_semantics=("arbitrary",),
            vmem_limit_bytes=VMEM_LIMIT),
        name="ffn_attn_out" if pre_proj else "ffn",
    )(*args)


def _qkv_kernel(x_ref, g_ref, w_ref, o_ref, *, q_scale):
    d = x_ref.shape[1]

    @pl.when(pl.program_id(1) == 0)
    def _():
        o_ref[...] = jnp.zeros_like(o_ref)

    @pl.when(pl.program_id(1) > 0)
    def _():
        h = _rmsnorm(x_ref[...], g_ref[...]).astype(BF16)
        o_ref[:, :d] = (_dot(h, w_ref[:, :d]) * q_scale).astype(o_ref.dtype)
        o_ref[:, d:] = _dot(h, w_ref[:, d:]).astype(o_ref.dtype)


def _qkv_proj(x, g, w_qkv, *, tm):
    bsz, s, d = x.shape
    n = w_qkv.shape[1]
    resident = dict(pipeline_mode=pl.Buffered(1))
    return pl.pallas_call(
        functools.partial(_qkv_kernel, q_scale=LOG2E / math.sqrt(SB_HEAD_DIM)),
        grid=(bsz, s // tm + 1),
        in_specs=[
            pl.BlockSpec((None, tm, d), lambda b, i: (b, jnp.maximum(i - 1, 0), 0)),
            pl.BlockSpec((1, d), lambda b, i: (0, 0), **resident),
            pl.BlockSpec((d, n), lambda b, i: (0, 0), **resident),
        ],
        out_specs=pl.BlockSpec((None, tm, n), lambda b, i: (b, i, 0)),
        out_shape=jax.ShapeDtypeStruct((bsz, s + tm, n), BF16),
        compiler_params=pltpu.CompilerParams(
            dimension_semantics=("arbitrary", "arbitrary"),
            vmem_limit_bytes=VMEM_LIMIT),
        name="qkv_proj",
    )(x, g, w_qkv)


def _sb_attn_kernel(q_ref, k_ref, v_ref, tri_ref, bias_ref, o_ref, acc_ref, c_ref, cmax_ref, *,
                    tiles, pad):
    tq, tk = Q_TILE, KEY_BLOCK
    step = pl.program_id(2)

    lane = lax.broadcasted_iota(jnp.int32, (tq, LANES), 1)
    head0, head1 = lane < SB_HEAD_DIM, lane >= SB_HEAD_DIM

    def scores(q, r0, first):
        zq = jnp.zeros_like(q)
        qs = jnp.concatenate([jnp.where(head0, q, zq), jnp.where(head1, q, zq)], axis=0)
        k = k_ref[pl.ds(r0, tk), :]
        z = lax.dot_general(qs, k, (((1,), (1,)), ((), ())), preferred_element_type=F32)
        if first:
            z = jnp.concatenate([z[:, :tk - LANES], z[:, tk - LANES:] + bias_ref[...]], axis=1)
        return z

    def weights(z):
        zneg, zpos = jnp.minimum(z, 0.0), jnp.maximum(z, 0.0)
        lg = jnp.log(1.0 + jnp.exp2(zneg - zpos)) * LOG2E
        sp = zpos + lg
        sfx = _dot(sp.astype(BF16), tri_ref[...])
        w = jnp.exp2((zneg - lg) + sfx)
        total = sfx[:, 0:1] - sp[:, 0:1]
        return w.astype(BF16), total

    def weighted_values(w, r0):
        return _dot(w, v_ref[pl.ds(r0, tk), :])

    rows2 = 2 * tq
    ends = [pl.multiple_of((step * tiles + g + 1) * tq + pad, tq) for g in range(tiles)]
    w, total = weights(jnp.concatenate(
        [scores(q_ref[g * tq:(g + 1) * tq, :], ends[g] - tk, True) for g in range(tiles)],
        axis=0))
    for g in range(tiles):
        pv = weighted_values(w[g * rows2:(g + 1) * rows2], ends[g] - tk)
        acc_ref[g] = jnp.where(head0, pv[:tq], pv[tq:])
        c = total[g * rows2:(g + 1) * rows2]
        c_ref[g] = c
        cmax_ref[g] = jnp.max(c)

    def more(carry):
        end, cm = carry
        return (end > pad) & (cm > EXIT_LOG)

    def walk(g, _):
        q = q_ref[pl.ds(pl.multiple_of(g * tq, tq), tq), :]

        def body(carry):
            end, _ = carry
            r0 = pl.multiple_of(end - tk, tq)
            w, total = weights(scores(q, r0, False))
            c = c_ref[g]
            pv = jnp.exp2(c) * weighted_values(w, r0)
            acc_ref[g] += jnp.where(head0, pv[:tq], pv[tq:])
            c = c + total
            c_ref[g] = c
            return end - tk, jnp.max(c)

        first_start = (step * tiles + g + 1) * tq + pad - tk
        lax.while_loop(more, body, (first_start, cmax_ref[g]))
        return 0

    lax.fori_loop(0, tiles, walk, 0)

    for g in range(tiles):
        o_ref[g * tq:(g + 1) * tq, :] = acc_ref[g].astype(o_ref.dtype)


def _sb_attention(qkv, *, d, pad, tiles):
    bsz, s_pad, _ = qkv.shape
    s = s_pad - pad
    pairs = d // LANES
    rows = tiles * Q_TILE
    assert pad % rows == 0 and s % rows == 0
    tq, tk = Q_TILE, KEY_BLOCK
    kr = lax.broadcasted_iota(jnp.int32, (tk, tk), 0)
    kc = lax.broadcasted_iota(jnp.int32, (tk, tk), 1)
    tri = jnp.where(kr > kc, -1.0, 0.0).astype(BF16)
    assert tq <= LANES and tk % LANES == 0
    mr = lax.broadcasted_iota(jnp.int32, (2 * tq, LANES), 0) & (tq - 1)
    mc = lax.broadcasted_iota(jnp.int32, (2 * tq, LANES), 1)
    bias = jnp.where(mc - (LANES - tq) < mr, 0.0, MASKED).astype(F32)
    const = lambda b, p, i: (0, 0)
    return pl.pallas_call(
        functools.partial(_sb_attn_kernel, tiles=tiles, pad=pad),
        grid=(bsz, pairs, s // rows),
        in_specs=[
            pl.BlockSpec((None, rows, LANES), lambda b, p, i: (b, i + pad // rows, p)),
            pl.BlockSpec((None, s_pad, LANES), lambda b, p, i: (b, 0, pairs + p)),
            pl.BlockSpec((None, s_pad, LANES), lambda b, p, i: (b, 0, 2 * pairs + p)),
            pl.BlockSpec((tk, tk), const),
            pl.BlockSpec((2 * tq, LANES), const),
        ],
        out_specs=pl.BlockSpec((None, rows, LANES), lambda b, p, i: (b, i, p)),
        out_shape=jax.ShapeDtypeStruct((bsz, s, d), BF16),
        scratch_shapes=[
            pltpu.VMEM((tiles, Q_TILE, LANES), F32),
            pltpu.VMEM((tiles, 2 * Q_TILE, 1), F32),
            pltpu.SMEM((tiles,), F32),
        ],
        compiler_params=pltpu.CompilerParams(
            dimension_semantics=("arbitrary", "arbitrary", "arbitrary"),
            vmem_limit_bytes=VMEM_LIMIT),
        name="sb_attention",
    )(qkv, qkv, qkv, tri, bias)


def kernel(x, norm_mix_g, norm_ffn_g, a_w_in, a_conv_w, a_conv_b, a_w_r, a_b_r, a_w_i, a_b_i,
           a_lambda, a_w_out, b_w_qkv, b_w_out, ffn_w_gate, ffn_w_up, ffn_w_down, final_g):
    bsz, s, d = x.shape
    m = bsz * s
    row = lambda v: v.reshape(1, -1)
    bf = lambda w: w.astype(BF16)

    x = _rglru_mixer(x, row(norm_mix_g[0]), bf(a_w_in[0]), a_conv_w[0], row(a_conv_b[0]),
                     bf(a_w_r[0]), row(a_b_r[0]), bf(a_w_i[0]), row(a_b_i[0]),
                     row(a_lambda[0]), bf(a_w_out[0]), tm=512)
    x = _ffn(x.reshape(m, d), row(norm_ffn_g[0]), bf(ffn_w_gate[0]), bf(ffn_w_up[0]),
             bf(ffn_w_down[0]), tm=1024, tf=256)

    tiles = 16
    pad = tiles * Q_TILE
    qkv = _qkv_proj(x.reshape(bsz, s, d), row(norm_mix_g[1]), bf(b_w_qkv[0]), tm=pad)
    att = _sb_attention(qkv, d=d, pad=pad, tiles=tiles)
    out = _ffn(x, row(norm_ffn_g[1]), bf(ffn_w_gate[1]), bf(ffn_w_up[1]), bf(ffn_w_down[1]),
               att=att.reshape(m, d), w_o=bf(b_w_out[0]), final_g=row(final_g), tm=1024, tf=256)
    return out.reshape(bsz, s, d)
```

```python
import functools
import math

import jax
import jax.numpy as jnp
from jax import lax
from jax.experimental import pallas as pl
from jax.experimental.pallas import tpu as pltpu

RMS_EPS = 1e-6
RG_C = 8.0
CONV_W = 4
RG_BLOCKS = 4
SB_HEAD_DIM = 64
LANES = 128
SUBLANES = 8
Q_TILE = 64
KEY_BLOCK = 256
LOG2E = 1.4426950408889634
GELU_C0 = math.sqrt(2.0 / math.pi)
EXIT_LOG = -130.0
MASKED = -1e30
VMEM_LIMIT = 48 * 1024 * 1024

BF16 = jnp.bfloat16
F32 = jnp.float32


def _rmsnorm(x, g):
    ms = jnp.mean(x * x, axis=-1, keepdims=True)
    return x * lax.rsqrt(ms + RMS_EPS) * g


def _dot(a, b):
    return jnp.dot(a, b, preferred_element_type=F32)


def _gelu_tanh(x):
    inner = x * (GELU_C0 + (GELU_C0 * 0.044715) * (x * x))
    hx = 0.5 * x
    return hx + hx * jnp.tanh(inner)


def _softplus(z):
    return jnp.maximum(z, 0.0) + jnp.log(1.0 + jnp.exp(-jnp.abs(z)))


def _rglru_kernel(x_ref, g_ref, win_ref, cw_ref, cb_ref, wr_ref, br_ref, wi_ref, bi_ref,
                  lam_ref, wout_ref, o_ref, xpad_ref, a_ref, u_ref, hs_ref, h_ref):
    tm, d = x_ref.shape
    bw = d // RG_BLOCKS
    pad = SUBLANES

    @pl.when(pl.program_id(1) == 0)
    def _():
        xpad_ref[0:pad, :] = jnp.zeros((pad, d), F32)
        h_ref[...] = jnp.zeros_like(h_ref)

    x = x_ref[...]
    hb = _rmsnorm(x, g_ref[...]).astype(BF16)
    proj = _dot(hb, win_ref[...])
    gate = _gelu_tanh(proj[:, :d])
    xpad_ref[pad:pad + tm, :] = proj[:, d:]

    xc = cb_ref[...] + cw_ref[CONV_W - 1:CONV_W, :] * xpad_ref[pad:pad + tm, :]
    for k in range(CONV_W - 1):
        off = pad - (CONV_W - 1) + k
        xc = xc + cw_ref[k:k + 1, :] * xpad_ref[off:off + tm, :]
    xpad_ref[0:pad, :] = xpad_ref[tm:tm + pad, :]

    xcb = xc.astype(BF16)
    lam = lam_ref[...]
    log_sig_lam = -_softplus(-lam)
    for n in range(RG_BLOCKS):
        sl = slice(n * bw, (n + 1) * bw)
        xb = xcb[:, sl]
        r = jax.nn.sigmoid(_dot(xb, wr_ref[n]) + br_ref[:, sl])
        i = jax.nn.sigmoid(_dot(xb, wi_ref[n]) + bi_ref[:, sl])
        log_a = RG_C * r * log_sig_lam[:, sl]
        a = jnp.exp(log_a)
        mult = jnp.exp2(jnp.log(jnp.maximum(-jnp.tanh(log_a) * (a * a + 1.0), 0.0))
                        * (0.5 * LOG2E))
        a_ref[:, sl] = a
        u_ref[:, sl] = mult * (i * xc[:, sl])

    row = lax.broadcasted_iota(jnp.int32, (SUBLANES, d), 0)

    def scan_group(gi, h):
        r0 = pl.multiple_of(gi * SUBLANES, SUBLANES)
        a = a_ref[pl.ds(r0, SUBLANES), :]
        u = u_ref[pl.ds(r0, SUBLANES), :]
        s = 1
        while s < SUBLANES:
            keep = row >= s
            a_sh = jnp.where(keep, pltpu.roll(a, s, 0), 1.0)
            u_sh = jnp.where(keep, pltpu.roll(u, s, 0), 0.0)
            u = a * u_sh + u
            a = a * a_sh
            s *= 2
        hs = a * h + u
        hs_ref[pl.ds(r0, SUBLANES), :] = hs
        return hs[SUBLANES - 1:SUBLANES, :]

    h_last = lax.fori_loop(0, tm // SUBLANES, scan_group, h_ref[...])
    h_ref[...] = h_last

    y = (hs_ref[...] * gate).astype(BF16)
    o_ref[...] = x + _dot(y, wout_ref[...])


def _rglru_mixer(x, g, w_in, conv_w, conv_b, w_r, b_r, w_i, b_i, lam, w_out, *, tm):
    bsz, s, d = x.shape
    bw = d // RG_BLOCKS
    const2 = lambda b, i: (0, 0)
    const3 = lambda b, i: (0, 0, 0)
    vec = pl.BlockSpec((1, d), const2)
    return pl.pallas_call(
        _rglru_kernel,
        grid=(bsz, s // tm),
        in_specs=[
            pl.BlockSpec((None, tm, d), lambda b, i: (b, i, 0)),
            vec,
            pl.BlockSpec((d, 2 * d), const2),
            pl.BlockSpec((CONV_W, d), const2),
            vec,
            pl.BlockSpec((RG_BLOCKS, bw, bw), const3),
            vec,
            pl.BlockSpec((RG_BLOCKS, bw, bw), const3),
            vec,
            vec,
            pl.BlockSpec((d, d), const2),
        ],
        out_specs=pl.BlockSpec((None, tm, d), lambda b, i: (b, i, 0)),
        out_shape=jax.ShapeDtypeStruct((bsz, s, d), F32),
        scratch_shapes=[
            pltpu.VMEM((tm + SUBLANES, d), F32),
            pltpu.VMEM((tm, d), F32),
            pltpu.VMEM((tm, d), F32),
            pltpu.VMEM((tm, d), F32),
            pltpu.VMEM((1, d), F32),
        ],
        compiler_params=pltpu.CompilerParams(
            dimension_semantics=("arbitrary", "arbitrary"),
            vmem_limit_bytes=VMEM_LIMIT),
        name="rglru_mixer",
    )(x, g, w_in, conv_w, conv_b, w_r, b_r, w_i, b_i, lam, w_out)


def _ffn_kernel(*refs, pre_proj, final_norm, nchunk):
    refs = list(refs)
    x_ref = refs.pop(0)
    if pre_proj:
        att_ref = refs.pop(0)
        wo_ref = refs.pop(0)
    g_ref, wg_ref, wu_ref, wd_ref = refs[:4]
    refs = refs[4:]
    if final_norm:
        fg_ref = refs.pop(0)
    o_ref, wg_s, wu_s, wd_s, hb_s = refs

    t = pl.program_id(0)

    def start_tile():
        x = x_ref[...]
        if pre_proj:
            x = x + _dot(att_ref[...], wo_ref[...].astype(BF16))
        o_ref[...] = x
        return _rmsnorm(x, g_ref[...]).astype(BF16)

    def add_chunk(h, c):
        gate = _dot(h, wg_s[c])
        up = _dot(h, wu_s[c])
        act = (gate * jax.nn.sigmoid(gate) * up).astype(BF16)
        o_ref[...] += _dot(act, wd_s[c])

    def finish_tile():
        if final_norm:
            o_ref[...] = _rmsnorm(o_ref[...], fg_ref[...])

    @pl.when(t < nchunk)
    def _():
        wg_s[t] = wg_ref[...].astype(BF16)
        wu_s[t] = wu_ref[...].astype(BF16)
        wd_s[t] = wd_ref[...].astype(BF16)

        @pl.when(t == 0)
        def _():
            hb_s[...] = start_tile()

        add_chunk(hb_s[...], t)

        @pl.when(t == nchunk - 1)
        def _():
            finish_tile()

    @pl.when(t >= nchunk)
    def _():
        h = start_tile()
        for c in range(nchunk):
            add_chunk(h, c)
        finish_tile()


def _ffn(x, g, w_gate, w_up, w_down, *, att=None, w_o=None, final_g=None, tm, tf):
    m, d = x.shape
    dff = w_gate.shape[1]
    assert dff % tf == 0 and m % tm == 0
    nchunk = dff // tf
    pre_proj = att is not None
    final_norm = final_g is not None
    resident = dict(pipeline_mode=pl.Buffered(1))
    tile_of = lambda t: jnp.maximum(t - (nchunk - 1), 0)
    chunk_of = lambda t: jnp.minimum(t, nchunk - 1)
    row = pl.BlockSpec((tm, d), lambda t: (tile_of(t), 0))
    vec = pl.BlockSpec((1, d), lambda t: (0, 0), **resident)
    args, specs = [x], [row]
    if pre_proj:
        args += [att, w_o]
        specs += [row, pl.BlockSpec((d, d), lambda t: (0, 0), **resident)]
    args += [g, w_gate, w_up, w_down]
    specs += [vec,
              pl.BlockSpec((d, tf), lambda t: (0, chunk_of(t))),
              pl.BlockSpec((d, tf), lambda t: (0, chunk_of(t))),
              pl.BlockSpec((tf, d), lambda t: (chunk_of(t), 0))]
    if final_norm:
        args.append(final_g)
        specs.append(vec)
    return pl.pallas_call(
        functools.partial(_ffn_kernel, pre_proj=pre_proj, final_norm=final_norm, nchunk=nchunk),
        grid=(nchunk + m // tm - 1,),
        in_specs=specs,
        out_specs=row,
        out_shape=jax.ShapeDtypeStruct((m, d), F32),
        scratch_shapes=[
            pltpu.VMEM((nchunk, d, tf), BF16),
            pltpu.VMEM((nchunk, d, tf), BF16),
            pltpu.VMEM((nchunk, tf, d), BF16),
            pltpu.VMEM((tm, d), BF16),
        ],
        compiler_params=pltpu.CompilerParams(
            dimension_semantics=("arbitrary",),
            vmem_limit_bytes=VMEM_LIMIT),
        name="ffn_attn_out" if pre_proj else "ffn",
    )(*args)


def _qkv_kernel(x_ref, g_ref, w_ref, o_ref, w_s, *, q_scale, nload, tiles_per_seq):
    d = x_ref.shape[1]
    t = pl.program_id(0)

    @pl.when(t < nload)
    def _():
        w_s[t] = w_ref[...].astype(BF16)
        o_ref[...] = jnp.zeros_like(o_ref)

    is_pad = lax.rem(t - (nload - 1), tiles_per_seq) == 0

    @pl.when((t >= nload) & is_pad)
    def _():
        o_ref[...] = jnp.zeros_like(o_ref)

    @pl.when((t >= nload) & jnp.logical_not(is_pad))
    def _():
        h = _rmsnorm(x_ref[...], g_ref[...]).astype(BF16)
        for c in range(nload):
            y = _dot(h, w_s[c])
            if c == 0:
                y = y * q_scale
            o_ref[:, c * d:(c + 1) * d] = y.astype(o_ref.dtype)


def _qkv_proj(x, g, w_qkv, *, tm):
    bsz, s, d = x.shape
    n = w_qkv.shape[1]
    nload = n // d
    tiles_per_seq = s // tm + 1
    resident = dict(pipeline_mode=pl.Buffered(1))
    item = lambda t: jnp.maximum(t - (nload - 1), 0)
    seq_of = lambda t: item(t) // tiles_per_seq
    tile_of = lambda t: item(t) % tiles_per_seq
    return pl.pallas_call(
        functools.partial(_qkv_kernel, q_scale=LOG2E / math.sqrt(SB_HEAD_DIM), nload=nload,
                          tiles_per_seq=tiles_per_seq),
        grid=(nload - 1 + bsz * tiles_per_seq,),
        in_specs=[
            pl.BlockSpec((None, tm, d), lambda t: (seq_of(t), jnp.maximum(tile_of(t) - 1, 0), 0)),
            pl.BlockSpec((1, d), lambda t: (0, 0), **resident),
            pl.BlockSpec((d, d), lambda t: (0, jnp.minimum(t, nload - 1))),
        ],
        out_specs=pl.BlockSpec((None, tm, n), lambda t: (seq_of(t), tile_of(t), 0)),
        out_shape=jax.ShapeDtypeStruct((bsz, s + tm, n), BF16),
        scratch_shapes=[pltpu.VMEM((nload, d, d), BF16)],
        compiler_params=pltpu.CompilerParams(
            dimension_semantics=("arbitrary",),
            vmem_limit_bytes=VMEM_LIMIT),
        name="qkv_proj",
    )(x, g, w_qkv)


def _sb_attn_kernel(q_ref, k_ref, v_ref, tri_ref, bias_ref, o_ref, acc_ref, c_ref, cmax_ref, *,
                    tiles, pad):
    tq, tk = Q_TILE, KEY_BLOCK
    step = pl.program_id(2)

    lane = lax.broadcasted_iota(jnp.int32, (tq, LANES), 1)
    head0, head1 = lane < SB_HEAD_DIM, lane >= SB_HEAD_DIM

    def scores(q, r0, first):
        zq = jnp.zeros_like(q)
        qs = jnp.concatenate([jnp.where(head0, q, zq), jnp.where(head1, q, zq)], axis=0)
        k = k_ref[pl.ds(r0, tk), :]
        z = lax.dot_general(qs, k, (((1,), (1,)), ((), ())), preferred_element_type=F32)
        if first:
            z = jnp.concatenate([z[:, :tk - LANES], z[:, tk - LANES:] + bias_ref[...]], axis=1)
        return z

    def weights(z):
        zneg, zpos = jnp.minimum(z, 0.0), jnp.maximum(z, 0.0)
        lg = jnp.log(1.0 + jnp.exp2(zneg - zpos)) * LOG2E
        sp = zpos + lg
        sfx = _dot(sp.astype(BF16), tri_ref[...])
        w = jnp.exp2((zneg - lg) + sfx)
        total = sfx[:, 0:1] - sp[:, 0:1]
        return w.astype(BF16), total

    def weighted_values(w, r0):
        return _dot(w, v_ref[pl.ds(r0, tk), :])

    rows2 = 2 * tq
    ends = [pl.multiple_of((step * tiles + g + 1) * tq + pad, tq) for g in range(tiles)]
    w, total = weights(jnp.concatenate(
        [scores(q_ref[g * tq:(g + 1) * tq, :], ends[g] - tk, True) for g in range(tiles)],
        axis=0))
    for g in range(tiles):
        pv = weighted_values(w[g * rows2:(g + 1) * rows2], ends[g] - tk)
        acc_ref[g] = jnp.where(head0, pv[:tq], pv[tq:])
        c = total[g * rows2:(g + 1) * rows2]
        c_ref[g] = c
        cmax_ref[g] = jnp.max(c)

    def more(carry):
        end, cm = carry
        return (end > pad) & (cm > EXIT_LOG)

    def walk(g, _):
        q = q_ref[pl.ds(pl.multiple_of(g * tq, tq), tq), :]

        def body(carry):
            end, _ = carry
            r0 = pl.multiple_of(end - tk, tq)
            w, total = weights(scores(q, r0, False))
            c = c_ref[g]
            pv = jnp.exp2(c) * weighted_values(w, r0)
            acc_ref[g] += jnp.where(head0, pv[:tq], pv[tq:])
            c = c + total
            c_ref[g] = c
            return end - tk, jnp.max(c)

        first_start = (step * tiles + g + 1) * tq + pad - tk
        lax.while_loop(more, body, (first_start, cmax_ref[g]))
        return 0

    lax.fori_loop(0, tiles, walk, 0)

    for g in range(tiles):
        o_ref[g * tq:(g + 1) * tq, :] = acc_ref[g].astype(o_ref.dtype)


def _sb_attention(qkv, *, d, pad, tiles):
    bsz, s_pad, _ = qkv.shape
    s = s_pad - pad
    pairs = d // LANES
    rows = tiles * Q_TILE
    assert pad % rows == 0 and s % rows == 0
    tq, tk = Q_TILE, KEY_BLOCK
    kr = lax.broadcasted_iota(jnp.int32, (tk, tk), 0)
    kc = lax.broadcasted_iota(jnp.int32, (tk, tk), 1)
    tri = jnp.where(kr > kc, -1.0, 0.0).astype(BF16)
    assert tq <= LANES and tk % LANES == 0
    mr = lax.broadcasted_iota(jnp.int32, (2 * tq, LANES), 0) & (tq - 1)
    mc = lax.broadcasted_iota(jnp.int32, (2 * tq, LANES), 1)
    bias = jnp.where(mc - (LANES - tq) < mr, 0.0, MASKED).astype(F32)
    const = lambda b, p, i: (0, 0)
    return pl.pallas_call(
        functools.partial(_sb_attn_kernel, tiles=tiles, pad=pad),
        grid=(bsz, pairs, s // rows),
        in_specs=[
            pl.BlockSpec((None, rows, LANES), lambda b, p, i: (b, i + pad // rows, p)),
            pl.BlockSpec((None, s_pad, LANES), lambda b, p, i: (b, 0, pairs + p)),
            pl.BlockSpec((None, s_pad, LANES), lambda b, p, i: (b, 0, 2 * pairs + p)),
            pl.BlockSpec((tk, tk), const),
            pl.BlockSpec((2 * tq, LANES), const),
        ],
        out_specs=pl.BlockSpec((None, rows, LANES), lambda b, p, i: (b, i, p)),
        out_shape=jax.ShapeDtypeStruct((bsz, s, d), BF16),
        scratch_shapes=[
            pltpu.VMEM((tiles, Q_TILE, LANES), F32),
            pltpu.VMEM((tiles, 2 * Q_TILE, 1), F32),
            pltpu.SMEM((tiles,), F32),
        ],
        compiler_params=pltpu.CompilerParams(
            dimension_semantics=("arbitrary", "arbitrary", "arbitrary"),
            vmem_limit_bytes=VMEM_LIMIT),
        name="sb_attention",
    )(qkv, qkv, qkv, tri, bias)


def kernel(x, norm_mix_g, norm_ffn_g, a_w_in, a_conv_w, a_conv_b, a_w_r, a_b_r, a_w_i, a_b_i,
           a_lambda, a_w_out, b_w_qkv, b_w_out, ffn_w_gate, ffn_w_up, ffn_w_down, final_g):
    bsz, s, d = x.shape
    m = bsz * s
    row = lambda v: v.reshape(1, -1)
    bf = lambda w: w.astype(BF16)

    x = _rglru_mixer(x, row(norm_mix_g[0]), bf(a_w_in[0]), a_conv_w[0], row(a_conv_b[0]),
                     bf(a_w_r[0]), row(a_b_r[0]), bf(a_w_i[0]), row(a_b_i[0]),
                     row(a_lambda[0]), bf(a_w_out[0]), tm=512)
    x = _ffn(x.reshape(m, d), row(norm_ffn_g[0]), ffn_w_gate[0], ffn_w_up[0], ffn_w_down[0],
             tm=1024, tf=256)

    tiles = 16
    pad = tiles * Q_TILE
    qkv = _qkv_proj(x.reshape(bsz, s, d), row(norm_mix_g[1]), b_w_qkv[0], tm=pad)
    att = _sb_attention(qkv, d=d, pad=pad, tiles=tiles)
    out = _ffn(x, row(norm_ffn_g[1]), ffn_w_gate[1], ffn_w_up[1], ffn_w_down[1],
               att=att.reshape(m, d), w_o=b_w_out[0], final_g=row(final_g), tm=512, tf=256)
    return out.reshape(bsz, s, d)
```

```python
import functools
import math

import jax
import jax.numpy as jnp
from jax import lax
from jax.experimental import pallas as pl
from jax.experimental.pallas import tpu as pltpu

RMS_EPS = 1e-6
RG_C = 8.0
CONV_W = 4
RG_BLOCKS = 4
SB_HEAD_DIM = 64
LANES = 128
SUBLANES = 8
ROW_CHUNK = 256
Q_TILE = 64
KEY_BLOCK = 256
LOG2E = 1.4426950408889634
GELU_C0 = math.sqrt(2.0 / math.pi)
EXIT_LOG = -130.0
MASKED = -1e30
VMEM_LIMIT = 48 * 1024 * 1024

BF16 = jnp.bfloat16
F32 = jnp.float32


def _rmsnorm(x, g):
    ms = jnp.mean(x * x, axis=-1, keepdims=True)
    return x * lax.rsqrt(ms + RMS_EPS) * g


def _dot(a, b):
    return jnp.dot(a, b, preferred_element_type=F32)


def _gelu_tanh(x):
    inner = x * (GELU_C0 + (GELU_C0 * 0.044715) * (x * x))
    hx = 0.5 * x
    return hx + hx * jnp.tanh(inner)


def _softplus(z):
    return jnp.maximum(z, 0.0) + jnp.log(1.0 + jnp.exp(-jnp.abs(z)))


def _rglru_kernel(x_ref, g_ref, win_ref, cw_ref, cb_ref, wr_ref, br_ref, wi_ref, bi_ref,
                  lam_ref, wout_ref, o_ref, xpad_ref, a_ref, u_ref, hs_ref, h_ref):
    tm, d = x_ref.shape
    bw = d // RG_BLOCKS
    pad = SUBLANES

    @pl.when(pl.program_id(1) == 0)
    def _():
        xpad_ref[0:pad, :] = jnp.zeros((pad, d), F32)
        h_ref[...] = jnp.zeros_like(h_ref)

    decay_rate = RG_C * -_softplus(-lam_ref[...])

    gates = []
    h = h_ref[...]
    for c0 in range(0, tm, ROW_CHUNK):
        rows = slice(c0, c0 + ROW_CHUNK)
        hb = _rmsnorm(x_ref[rows, :], g_ref[...]).astype(BF16)
        xpad_ref[pad + c0:pad + c0 + ROW_CHUNK, :] = _dot(hb, win_ref[:, d:])

        xc = cb_ref[...] + cw_ref[CONV_W - 1:CONV_W, :] * xpad_ref[pad + c0:pad + c0 + ROW_CHUNK, :]
        for k in range(CONV_W - 1):
            off = pad + c0 - (CONV_W - 1) + k
            xc = xc + cw_ref[k:k + 1, :] * xpad_ref[off:off + ROW_CHUNK, :]

        xcb = xc.astype(BF16)
        for n in range(RG_BLOCKS):
            sl = slice(n * bw, (n + 1) * bw)
            r = jax.nn.sigmoid(_dot(xcb[:, sl], wr_ref[n]) + br_ref[:, sl])
            i = jax.nn.sigmoid(_dot(xcb[:, sl], wi_ref[n]) + bi_ref[:, sl])
            log_a = r * decay_rate[:, sl]
            a = jnp.exp(log_a)
            mult = jnp.exp2(jnp.log(jnp.maximum(jnp.tanh(log_a) * (-1.0 - a * a), 0.0))
                            * (0.5 * LOG2E))
            a_ref[rows, sl] = a
            u_ref[rows, sl] = mult * (i * xc[:, sl])

        for t in range(c0, c0 + ROW_CHUNK):
            h = a_ref[t:t + 1, :] * h + u_ref[t:t + 1, :]
            hs_ref[t:t + 1, :] = h

        gates.append(_gelu_tanh(_dot(hb, win_ref[:, :d])))
    h_ref[...] = h
    xpad_ref[0:pad, :] = xpad_ref[tm:tm + pad, :]

    for c, c0 in enumerate(range(0, tm, ROW_CHUNK)):
        rows = slice(c0, c0 + ROW_CHUNK)
        y = (hs_ref[rows, :] * gates[c]).astype(BF16)
        o_ref[rows, :] = x_ref[rows, :] + _dot(y, wout_ref[...])


def _rglru_mixer(x, g, w_in, conv_w, conv_b, w_r, b_r, w_i, b_i, lam, w_out, *, tm):
    bsz, s, d = x.shape
    bw = d // RG_BLOCKS
    const2 = lambda b, i: (0, 0)
    const3 = lambda b, i: (0, 0, 0)
    vec = pl.BlockSpec((1, d), const2)
    return pl.pallas_call(
        _rglru_kernel,
        grid=(bsz, s // tm),
        in_specs=[
            pl.BlockSpec((None, tm, d), lambda b, i: (b, i, 0)),
            vec,
            pl.BlockSpec((d, 2 * d), const2),
            pl.BlockSpec((CONV_W, d), const2),
            vec,
            pl.BlockSpec((RG_BLOCKS, bw, bw), const3),
            vec,
            pl.BlockSpec((RG_BLOCKS, bw, bw), const3),
            vec,
            vec,
            pl.BlockSpec((d, d), const2),
        ],
        out_specs=pl.BlockSpec((None, tm, d), lambda b, i: (b, i, 0)),
        out_shape=jax.ShapeDtypeStruct((bsz, s, d), F32),
        scratch_shapes=[
            pltpu.VMEM((tm + SUBLANES, d), F32),
            pltpu.VMEM((tm, d), F32),
            pltpu.VMEM((tm, d), F32),
            pltpu.VMEM((tm, d), F32),
            pltpu.VMEM((1, d), F32),
        ],
        compiler_params=pltpu.CompilerParams(
            dimension_semantics=("arbitrary", "arbitrary"),
            vmem_limit_bytes=VMEM_LIMIT),
        name="rglru_mixer",
    )(x, g, w_in, conv_w, conv_b, w_r, b_r, w_i, b_i, lam, w_out)


def _ffn_kernel(*refs, pre_proj, final_norm, nchunk):
    refs = list(refs)
    x_ref = refs.pop(0)
    if pre_proj:
        att_ref = refs.pop(0)
        wo_ref = refs.pop(0)
    g_ref, wg_ref, wu_ref, wd_ref = refs[:4]
    refs = refs[4:]
    if final_norm:
        fg_ref = refs.pop(0)
    o_ref, wg_s, wu_s, wd_s, hb_s = refs

    t = pl.program_id(0)

    def start_tile():
        x = x_ref[...]
        if pre_proj:
            x = x + _dot(att_ref[...], wo_ref[...].astype(BF16))
        o_ref[...] = x
        return _rmsnorm(x, g_ref[...]).astype(BF16)

    def add_chunk(h, c):
        gate = _dot(h, wg_s[c])
        up = _dot(h, wu_s[c])
        act = (gate * jax.nn.sigmoid(gate) * up).astype(BF16)
        o_ref[...] += _dot(act, wd_s[c])

    def finish_tile():
        if final_norm:
            o_ref[...] = _rmsnorm(o_ref[...], fg_ref[...])

    @pl.when(t < nchunk)
    def _():
        wg_s[t] = wg_ref[...].astype(BF16)
        wu_s[t] = wu_ref[...].astype(BF16)
        wd_s[t] = wd_ref[...].astype(BF16)

        @pl.when(t == 0)
        def _():
            hb_s[...] = start_tile()

        add_chunk(hb_s[...], t)

        @pl.when(t == nchunk - 1)
        def _():
            finish_tile()

    @pl.when(t >= nchunk)
    def _():
        h = start_tile()
        for c in range(nchunk):
            add_chunk(h, c)
        finish_tile()


def _ffn(x, g, w_gate, w_up, w_down, layer, *, att=None, w_o=None, final_g=None, tm, tf):
    m, d = x.shape
    dff = w_gate.shape[2]
    assert dff % tf == 0 and m % tm == 0
    nchunk = dff // tf
    pre_proj = att is not None
    final_norm = final_g is not None
    resident = dict(pipeline_mode=pl.Buffered(1))
    tile_of = lambda t: jnp.maximum(t - (nchunk - 1), 0)
    chunk_of = lambda t: jnp.minimum(t, nchunk - 1)
    row = pl.BlockSpec((tm, d), lambda t: (tile_of(t), 0))
    vec = pl.BlockSpec((1, d), lambda t: (0, 0), **resident)
    args, specs = [x], [row]
    if pre_proj:
        args += [att, w_o]
        specs += [row, pl.BlockSpec((d, d), lambda t: (0, 0), **resident)]
    args += [g, w_gate, w_up, w_down]
    specs += [vec,
              pl.BlockSpec((None, d, tf), lambda t: (layer, 0, chunk_of(t))),
              pl.BlockSpec((None, d, tf), lambda t: (layer, 0, chunk_of(t))),
              pl.BlockSpec((None, tf, d), lambda t: (layer, chunk_of(t), 0))]
    if final_norm:
        args.append(final_g)
        specs.append(vec)
    return pl.pallas_call(
        functools.partial(_ffn_kernel, pre_proj=pre_proj, final_norm=final_norm, nchunk=nchunk),
        grid=(nchunk + m // tm - 1,),
        in_specs=specs,
        out_specs=row,
        out_shape=jax.ShapeDtypeStruct((m, d), F32),
        scratch_shapes=[
            pltpu.VMEM((nchunk, d, tf), BF16),
            pltpu.VMEM((nchunk, d, tf), BF16),
            pltpu.VMEM((nchunk, tf, d), BF16),
            pltpu.VMEM((tm, d), BF16),
        ],
        compiler_params=pltpu.CompilerParams(
            dimension_semantics=("arbitrary",),
            vmem_limit_bytes=VMEM_LIMIT),
        name="ffn_attn_out" if pre_proj else "ffn",
    )(*args)


def _qkv_kernel(x_ref, g_ref, w_ref, o_ref, w_s, *, q_scale, nload, tiles_per_seq):
    d = x_ref.shape[1]
    t = pl.program_id(0)

    @pl.when(t < nload)
    def _():
        w_s[t] = w_ref[...].astype(BF16)
        o_ref[...] = jnp.zeros_like(o_ref)

    is_pad = lax.rem(t - (nload - 1), tiles_per_seq) == 0

    @pl.when((t >= nload) & is_pad)
    def _():
        o_ref[...] = jnp.zeros_like(o_ref)

    @pl.when((t >= nload) & jnp.logical_not(is_pad))
    def _():
        h = _rmsnorm(x_ref[...], g_ref[...]).astype(BF16)
        for c in range(nload):
            y = _dot(h, w_s[c])
            if c == 0:
                y = y * q_scale
            o_ref[:, c * d:(c + 1) * d] = y.astype(o_ref.dtype)


def _qkv_proj(x, g, w_qkv, *, tm):
    bsz, s, d = x.shape
    n = w_qkv.shape[1]
    nload = n // d
    tiles_per_seq = s // tm + 1
    resident = dict(pipeline_mode=pl.Buffered(1))
    item = lambda t: jnp.maximum(t - (nload - 1), 0)
    seq_of = lambda t: item(t) // tiles_per_seq
    tile_of = lambda t: item(t) % tiles_per_seq
    return pl.pallas_call(
        functools.partial(_qkv_kernel, q_scale=LOG2E / math.sqrt(SB_HEAD_DIM), nload=nload,
                          tiles_per_seq=tiles_per_seq),
        grid=(nload - 1 + bsz * tiles_per_seq,),
        in_specs=[
            pl.BlockSpec((None, tm, d), lambda t: (seq_of(t), jnp.maximum(tile_of(t) - 1, 0), 0)),
            pl.BlockSpec((1, d), lambda t: (0, 0), **resident),
            pl.BlockSpec((d, d), lambda t: (0, jnp.minimum(t, nload - 1))),
        ],
        out_specs=pl.BlockSpec((None, tm, n), lambda t: (seq_of(t), tile_of(t), 0)),
        out_shape=jax.ShapeDtypeStruct((bsz, s + tm, n), BF16),
        scratch_shapes=[pltpu.VMEM((nload, d, d), BF16)],
        compiler_params=pltpu.CompilerParams(
            dimension_semantics=("arbitrary",),
            vmem_limit_bytes=VMEM_LIMIT),
        name="qkv_proj",
    )(x, g, w_qkv)


def _sb_attn_kernel(q_ref, k_ref, v_ref, tri_ref, bias_ref, o_ref, acc_ref, c_ref, cmax_ref, *,
                    tiles, pad):
    tq, tk = Q_TILE, KEY_BLOCK
    step = pl.program_id(2)

    lane = lax.broadcasted_iota(jnp.int32, (tq, LANES), 1)
    head0, head1 = lane < SB_HEAD_DIM, lane >= SB_HEAD_DIM

    def scores(q, r0, first):
        zq = jnp.zeros_like(q)
        qs = jnp.concatenate([jnp.where(head0, q, zq), jnp.where(head1, q, zq)], axis=0)
        k = k_ref[pl.ds(r0, tk), :]
        z = lax.dot_general(qs, k, (((1,), (1,)), ((), ())), preferred_element_type=F32)
        if first:
            z = jnp.concatenate([z[:, :tk - LANES], z[:, tk - LANES:] + bias_ref[...]], axis=1)
        return z

    def weights(z):
        zneg, zpos = jnp.minimum(z, 0.0), jnp.maximum(z, 0.0)
        lg = jnp.log(1.0 + jnp.exp2(zneg - zpos)) * LOG2E
        sp = zpos + lg
        sfx = _dot(sp.astype(BF16), tri_ref[...])
        w = jnp.exp2((zneg - lg) + sfx)
        total = sfx[:, 0:1] - sp[:, 0:1]
        return w.astype(BF16), total

    def weighted_values(w, r0):
        return _dot(w, v_ref[pl.ds(r0, tk), :])

    rows2 = 2 * tq
    ends = [pl.multiple_of((step * tiles + g + 1) * tq + pad, tq) for g in range(tiles)]
    w, total = weights(jnp.concatenate(
        [scores(q_ref[g * tq:(g + 1) * tq, :], ends[g] - tk, True) for g in range(tiles)],
        axis=0))
    for g in range(tiles):
        pv = weighted_values(w[g * rows2:(g + 1) * rows2], ends[g] - tk)
        acc_ref[g] = jnp.where(head0, pv[:tq], pv[tq:])
        c = total[g * rows2:(g + 1) * rows2]
        c_ref[g] = c
        cmax_ref[g] = jnp.max(c)

    def more(carry):
        end, cm = carry
        return (end > pad) & (cm > EXIT_LOG)

    def walk(g, _):
        q = q_ref[pl.ds(pl.multiple_of(g * tq, tq), tq), :]

        def body(carry):
            end, _ = carry
            r0 = pl.multiple_of(end - tk, tq)
            w, total = weights(scores(q, r0, False))
            c = c_ref[g]
            pv = jnp.exp2(c) * weighted_values(w, r0)
            acc_ref[g] += jnp.where(head0, pv[:tq], pv[tq:])
            c = c + total
            c_ref[g] = c
            return end - tk, jnp.max(c)

        first_start = (step * tiles + g + 1) * tq + pad - tk
        lax.while_loop(more, body, (first_start, cmax_ref[g]))
        return 0

    lax.fori_loop(0, tiles, walk, 0)

    for g in range(tiles):
        o_ref[g * tq:(g + 1) * tq, :] = acc_ref[g].astype(o_ref.dtype)


def _sb_attention(qkv, *, d, pad, tiles):
    bsz, s_pad, _ = qkv.shape
    s = s_pad - pad
    pairs = d // LANES
    rows = tiles * Q_TILE
    assert pad % rows == 0 and s % rows == 0
    tq, tk = Q_TILE, KEY_BLOCK
    kr = lax.broadcasted_iota(jnp.int32, (tk, tk), 0)
    kc = lax.broadcasted_iota(jnp.int32, (tk, tk), 1)
    tri = jnp.where(kr > kc, -1.0, 0.0).astype(BF16)
    assert tq <= LANES and tk % LANES == 0
    mr = lax.broadcasted_iota(jnp.int32, (2 * tq, LANES), 0) & (tq - 1)
    mc = lax.broadcasted_iota(jnp.int32, (2 * tq, LANES), 1)
    bias = jnp.where(mc - (LANES - tq) < mr, 0.0, MASKED).astype(F32)
    const = lambda b, p, i: (0, 0)
    return pl.pallas_call(
        functools.partial(_sb_attn_kernel, tiles=tiles, pad=pad),
        grid=(bsz, pairs, s // rows),
        in_specs=[
            pl.BlockSpec((None, rows, LANES), lambda b, p, i: (b, i + pad // rows, p)),
            pl.BlockSpec((None, s_pad, LANES), lambda b, p, i: (b, 0, pairs + p)),
            pl.BlockSpec((None, s_pad, LANES), lambda b, p, i: (b, 0, 2 * pairs + p)),
            pl.BlockSpec((tk, tk), const),
            pl.BlockSpec((2 * tq, LANES), const),
        ],
        out_specs=pl.BlockSpec((None, rows, LANES), lambda b, p, i: (b, i, p)),
        out_shape=jax.ShapeDtypeStruct((bsz, s, d), BF16),
        scratch_shapes=[
            pltpu.VMEM((tiles, Q_TILE, LANES), F32),
            pltpu.VMEM((tiles, 2 * Q_TILE, 1), F32),
            pltpu.SMEM((tiles,), F32),
        ],
        compiler_params=pltpu.CompilerParams(
            dimension_semantics=("arbitrary", "arbitrary", "arbitrary"),
            vmem_limit_bytes=VMEM_LIMIT),
        name="sb_attention",
    )(qkv, qkv, qkv, tri, bias)


def kernel(x, norm_mix_g, norm_ffn_g, a_w_in, a_conv_w, a_conv_b, a_w_r, a_b_r, a_w_i, a_b_i,
           a_lambda, a_w_out, b_w_qkv, b_w_out, ffn_w_gate, ffn_w_up, ffn_w_down, final_g):
    bsz, s, d = x.shape
    m = bsz * s
    row = lambda v: v.reshape(1, -1)
    bf = lambda w: w.astype(BF16)

    x = _rglru_mixer(x, row(norm_mix_g[0]), bf(a_w_in[0]), a_conv_w[0], row(a_conv_b[0]),
                     bf(a_w_r[0]), row(a_b_r[0]), bf(a_w_i[0]), row(a_b_i[0]),
                     row(a_lambda[0]), bf(a_w_out[0]), tm=512)
    x = _ffn(x.reshape(m, d), row(norm_ffn_g[0]), ffn_w_gate, ffn_w_up, ffn_w_down, 0,
             tm=1024, tf=256)

    tiles = 16
    pad = tiles * Q_TILE
    qkv = _qkv_proj(x.reshape(bsz, s, d), row(norm_mix_g[1]), b_w_qkv[0], tm=pad)
    att = _sb_attention(qkv, d=d, pad=pad, tiles=tiles)
    out = _ffn(x, row(norm_ffn_g[1]), ffn_w_gate, ffn_w_up, ffn_w_down, 1,
               att=att.reshape(m, d), w_o=b_w_out[0], final_g=row(final_g), tm=512, tf=256)
    return out.reshape(bsz, s, d)
```

```python
import functools
import math

import jax
import jax.numpy as jnp
from jax import lax
from jax.experimental import pallas as pl
from jax.experimental.pallas import tpu as pltpu

RMS_EPS = 1e-6
RG_C = 8.0
CONV_W = 4
RG_BLOCKS = 4
SB_HEAD_DIM = 64
LANES = 128
SUBLANES = 8
ROW_CHUNK = 256
Q_TILE = 64
KEY_BLOCK = 256
LOG2E = 1.4426950408889634
GELU_C0 = math.sqrt(2.0 / math.pi)
EXIT_LOG = -130.0
MASKED = -1e30
VMEM_LIMIT = 48 * 1024 * 1024

BF16 = jnp.bfloat16
F32 = jnp.float32


def _rmsnorm(x, g):
    ms = jnp.mean(x * x, axis=-1, keepdims=True)
    return x * lax.rsqrt(ms + RMS_EPS) * g


def _dot(a, b):
    return jnp.dot(a, b, preferred_element_type=F32)


def _gelu_tanh(x):
    inner = x * (GELU_C0 + (GELU_C0 * 0.044715) * (x * x))
    hx = 0.5 * x
    return hx + hx * jnp.tanh(inner)


def _softplus(z):
    return jnp.maximum(z, 0.0) + jnp.log(1.0 + jnp.exp(-jnp.abs(z)))


def _rglru_kernel(x_ref, g_ref, win_ref, cw_ref, cb_ref, wr_ref, br_ref, wi_ref, bi_ref,
                  lam_ref, wout_ref, o_ref, xpad_ref, a_ref, u_ref, hs_ref, h_ref):
    tm, d = x_ref.shape
    bw = d // RG_BLOCKS
    pad = SUBLANES

    @pl.when(pl.program_id(1) == 0)
    def _():
        xpad_ref[0:pad, :] = jnp.zeros((pad, d), F32)
        h_ref[...] = jnp.zeros_like(h_ref)

    decay_rate = RG_C * -_softplus(-lam_ref[...])

    gates = []
    h = h_ref[...]
    for c0 in range(0, tm, ROW_CHUNK):
        rows = slice(c0, c0 + ROW_CHUNK)
        hb = _rmsnorm(x_ref[rows, :], g_ref[...]).astype(BF16)
        xpad_ref[pad + c0:pad + c0 + ROW_CHUNK, :] = _dot(hb, win_ref[:, d:])

        xc = cb_ref[...] + cw_ref[CONV_W - 1:CONV_W, :] * xpad_ref[pad + c0:pad + c0 + ROW_CHUNK, :]
        for k in range(CONV_W - 1):
            off = pad + c0 - (CONV_W - 1) + k
            xc = xc + cw_ref[k:k + 1, :] * xpad_ref[off:off + ROW_CHUNK, :]

        xcb = xc.astype(BF16)
        for n in range(RG_BLOCKS):
            sl = slice(n * bw, (n + 1) * bw)
            r = jax.nn.sigmoid(_dot(xcb[:, sl], wr_ref[n]) + br_ref[:, sl])
            i = jax.nn.sigmoid(_dot(xcb[:, sl], wi_ref[n]) + bi_ref[:, sl])
            log_a = r * decay_rate[:, sl]
            a = jnp.exp(log_a)
            mult = jnp.exp2(jnp.log(jnp.maximum(jnp.tanh(log_a) * (-1.0 - a * a), 0.0))
                            * (0.5 * LOG2E))
            a_ref[rows, sl] = a
            u_ref[rows, sl] = mult * (i * xc[:, sl])

        for t in range(c0, c0 + ROW_CHUNK):
            h = a_ref[t:t + 1, :] * h + u_ref[t:t + 1, :]
            hs_ref[t:t + 1, :] = h

        gates.append(_gelu_tanh(_dot(hb, win_ref[:, :d])))
    h_ref[...] = h
    xpad_ref[0:pad, :] = xpad_ref[tm:tm + pad, :]

    for c, c0 in enumerate(range(0, tm, ROW_CHUNK)):
        rows = slice(c0, c0 + ROW_CHUNK)
        y = (hs_ref[rows, :] * gates[c]).astype(BF16)
        o_ref[rows, :] = x_ref[rows, :] + _dot(y, wout_ref[...])


def _rglru_mixer(x, g, w_in, conv_w, conv_b, w_r, b_r, w_i, b_i, lam, w_out, *, tm):
    bsz, s, d = x.shape
    bw = d // RG_BLOCKS
    const2 = lambda b, i: (0, 0)
    const3 = lambda b, i: (0, 0, 0)
    vec = pl.BlockSpec((1, d), const2)
    return pl.pallas_call(
        _rglru_kernel,
        grid=(bsz, s // tm),
        in_specs=[
            pl.BlockSpec((None, tm, d), lambda b, i: (b, i, 0)),
            vec,
            pl.BlockSpec((d, 2 * d), const2),
            pl.BlockSpec((CONV_W, d), const2),
            vec,
            pl.BlockSpec((RG_BLOCKS, bw, bw), const3),
            vec,
            pl.BlockSpec((RG_BLOCKS, bw, bw), const3),
            vec,
            vec,
            pl.BlockSpec((d, d), const2),
        ],
        out_specs=pl.BlockSpec((None, tm, d), lambda b, i: (b, i, 0)),
        out_shape=jax.ShapeDtypeStruct((bsz, s, d), F32),
        scratch_shapes=[
            pltpu.VMEM((tm + SUBLANES, d), F32),
            pltpu.VMEM((tm, d), F32),
            pltpu.VMEM((tm, d), F32),
            pltpu.VMEM((tm, d), F32),
            pltpu.VMEM((1, d), F32),
        ],
        compiler_params=pltpu.CompilerParams(
            dimension_semantics=("arbitrary", "arbitrary"),
            vmem_limit_bytes=VMEM_LIMIT),
        name="rglru_mixer",
    )(x, g, w_in, conv_w, conv_b, w_r, b_r, w_i, b_i, lam, w_out)


def _ffn_kernel(*refs, pre_proj, final_norm, nchunk):
    refs = list(refs)
    x_ref = refs.pop(0)
    if pre_proj:
        att_ref = refs.pop(0)
        wo_ref = refs.pop(0)
    g_ref, wg_ref, wu_ref, wd_ref = refs[:4]
    refs = refs[4:]
    if final_norm:
        fg_ref = refs.pop(0)
    o_ref, wg_s, wu_s, wd_s, hb_s = refs

    t = pl.program_id(0)

    def start_tile():
        x = x_ref[...]
        if pre_proj:
            x = x + _dot(att_ref[...], wo_ref[...].astype(BF16))
        o_ref[...] = x
        return _rmsnorm(x, g_ref[...]).astype(BF16)

    def add_chunk(h, c):
        gate = _dot(h, wg_s[c])
        up = _dot(h, wu_s[c])
        act = (gate * jax.nn.sigmoid(gate) * up).astype(BF16)
        o_ref[...] += _dot(act, wd_s[c])

    def finish_tile():
        if final_norm:
            o_ref[...] = _rmsnorm(o_ref[...], fg_ref[...])

    @pl.when(t < nchunk)
    def _():
        wg_s[t] = wg_ref[...].astype(BF16)
        wu_s[t] = wu_ref[...].astype(BF16)
        wd_s[t] = wd_ref[...].astype(BF16)

        @pl.when(t == 0)
        def _():
            hb_s[...] = start_tile()

        add_chunk(hb_s[...], t)

        @pl.when(t == nchunk - 1)
        def _():
            finish_tile()

    @pl.when(t >= nchunk)
    def _():
        h = start_tile()
        for c in range(nchunk):
            add_chunk(h, c)
        finish_tile()


def _ffn(x, g, w_gate, w_up, w_down, layer, *, att=None, w_o=None, final_g=None, tm, tf):
    m, d = x.shape
    dff = w_gate.shape[2]
    assert dff % tf == 0 and m % tm == 0
    nchunk = dff // tf
    pre_proj = att is not None
    final_norm = final_g is not None
    resident = dict(pipeline_mode=pl.Buffered(1))
    tile_of = lambda t: jnp.maximum(t - (nchunk - 1), 0)
    chunk_of = lambda t: jnp.minimum(t, nchunk - 1)
    row = pl.BlockSpec((tm, d), lambda t: (tile_of(t), 0))
    vec = pl.BlockSpec((1, d), lambda t: (0, 0), **resident)
    args, specs = [x], [row]
    if pre_proj:
        args += [att, w_o]
        specs += [row, pl.BlockSpec((d, d), lambda t: (0, 0), **resident)]
    args += [g, w_gate, w_up, w_down]
    specs += [vec,
              pl.BlockSpec((None, d, tf), lambda t: (layer, 0, chunk_of(t))),
              pl.BlockSpec((None, d, tf), lambda t: (layer, 0, chunk_of(t))),
              pl.BlockSpec((None, tf, d), lambda t: (layer, chunk_of(t), 0))]
    if final_norm:
        args.append(final_g)
        specs.append(vec)
    return pl.pallas_call(
        functools.partial(_ffn_kernel, pre_proj=pre_proj, final_norm=final_norm, nchunk=nchunk),
        grid=(nchunk + m // tm - 1,),
        in_specs=specs,
        out_specs=row,
        out_shape=jax.ShapeDtypeStruct((m, d), F32),
        scratch_shapes=[
            pltpu.VMEM((nchunk, d, tf), BF16),
            pltpu.VMEM((nchunk, d, tf), BF16),
            pltpu.VMEM((nchunk, tf, d), BF16),
            pltpu.VMEM((tm, d), BF16),
        ],
        compiler_params=pltpu.CompilerParams(
            dimension_semantics=("arbitrary",),
            vmem_limit_bytes=VMEM_LIMIT),
        name="ffn_attn_out" if pre_proj else "ffn",
    )(*args)


def _qkv_kernel(x_ref, g_ref, w_ref, o_ref, w_s, *, q_scale, nload, pad_tiles, tiles_per_seq):
    d = x_ref.shape[1]
    t = pl.program_id(0)

    @pl.when(t < nload)
    def _():
        w_s[t] = w_ref[...].astype(BF16)

    item = jnp.maximum(t - (nload - pad_tiles), 0)
    is_pad = lax.rem(item, tiles_per_seq) < pad_tiles

    @pl.when(is_pad)
    def _():
        o_ref[...] = jnp.zeros_like(o_ref)

    @pl.when(jnp.logical_not(is_pad))
    def _():
        h = _rmsnorm(x_ref[...], g_ref[...]).astype(BF16)
        for c in range(nload):
            y = _dot(h, w_s[c])
            if c == 0:
                y = y * q_scale
            o_ref[:, c * d:(c + 1) * d] = y.astype(o_ref.dtype)


def _qkv_proj(x, g, w_qkv, *, tm, pad):
    bsz, s, d = x.shape
    n = w_qkv.shape[1]
    nload = n // d
    pad_tiles = pad // tm
    assert pad % tm == 0 and s % tm == 0 and 1 <= pad_tiles <= nload
    tiles_per_seq = s // tm + pad_tiles
    resident = dict(pipeline_mode=pl.Buffered(1))
    item = lambda t: jnp.maximum(t - (nload - pad_tiles), 0)
    seq_of = lambda t: item(t) // tiles_per_seq
    tile_of = lambda t: item(t) % tiles_per_seq
    return pl.pallas_call(
        functools.partial(_qkv_kernel, q_scale=LOG2E / math.sqrt(SB_HEAD_DIM), nload=nload,
                          pad_tiles=pad_tiles, tiles_per_seq=tiles_per_seq),
        grid=(nload - pad_tiles + bsz * tiles_per_seq,),
        in_specs=[
            pl.BlockSpec((None, tm, d),
                         lambda t: (seq_of(t), jnp.maximum(tile_of(t) - pad_tiles, 0), 0)),
            pl.BlockSpec((1, d), lambda t: (0, 0), **resident),
            pl.BlockSpec((d, d), lambda t: (0, jnp.minimum(t, nload - 1))),
        ],
        out_specs=pl.BlockSpec((None, tm, n), lambda t: (seq_of(t), tile_of(t), 0)),
        out_shape=jax.ShapeDtypeStruct((bsz, s + pad, n), BF16),
        scratch_shapes=[pltpu.VMEM((nload, d, d), BF16)],
        compiler_params=pltpu.CompilerParams(
            dimension_semantics=("arbitrary",),
            vmem_limit_bytes=VMEM_LIMIT),
        name="qkv_proj",
    )(x, g, w_qkv)


def _sb_attn_kernel(q_ref, k_ref, v_ref, tri_ref, bias_ref, o_ref, acc_ref, c_ref, *, tiles, pad):
    tq, tk = Q_TILE, KEY_BLOCK
    step = pl.program_id(2)

    lane = lax.broadcasted_iota(jnp.int32, (tq, LANES), 1)
    head0, head1 = lane < SB_HEAD_DIM, lane >= SB_HEAD_DIM

    def scores(q, r0, first):
        zq = jnp.zeros_like(q)
        qs = jnp.concatenate([jnp.where(head0, q, zq), jnp.where(head1, q, zq)], axis=0)
        k = k_ref[pl.ds(r0, tk), :]
        z = lax.dot_general(qs, k, (((1,), (1,)), ((), ())), preferred_element_type=F32)
        if first:
            z = jnp.concatenate([z[:, :tk - LANES], z[:, tk - LANES:] + bias_ref[...]], axis=1)
        return z

    def weights(z):
        zneg, zpos = jnp.minimum(z, 0.0), jnp.maximum(z, 0.0)
        lg = jnp.log(1.0 + jnp.exp2(zneg - zpos)) * LOG2E
        sp = zpos + lg
        sfx = _dot(sp.astype(BF16), tri_ref[...])
        w = jnp.exp2((zneg - lg) + sfx)
        total = sfx[:, 0:1] - sp[:, 0:1]
        return w.astype(BF16), total

    def weighted_values(w, r0):
        return _dot(w, v_ref[pl.ds(r0, tk), :])

    rows2 = 2 * tq
    ends = [pl.multiple_of((step * tiles + g + 1) * tq + pad, tq) for g in range(tiles)]
    w, total = weights(jnp.concatenate(
        [scores(q_ref[g * tq:(g + 1) * tq, :], ends[g] - tk, True) for g in range(tiles)],
        axis=0))
    for g in range(tiles):
        pv = weighted_values(w[g * rows2:(g + 1) * rows2], ends[g] - tk)
        acc = jnp.where(head0, pv[:tq], pv[tq:])
        o_ref[g * tq:(g + 1) * tq, :] = acc.astype(o_ref.dtype)
        acc_ref[g] = acc
        c_ref[g] = total[g * rows2:(g + 1) * rows2]

    @pl.when(jnp.max(total) > EXIT_LOG)
    def _():
        def more(carry):
            end, cm = carry
            return (end > pad) & (cm > EXIT_LOG)

        def walk(g, _):
            q0 = pl.multiple_of(g * tq, tq)
            q = q_ref[pl.ds(q0, tq), :]

            def body(carry):
                end, _ = carry
                r0 = pl.multiple_of(end - tk, tq)
                w, total = weights(scores(q, r0, False))
                c = c_ref[g]
                pv = jnp.exp2(c) * weighted_values(w, r0)
                acc_ref[g] += jnp.where(head0, pv[:tq], pv[tq:])
                c = c + total
                c_ref[g] = c
                return end - tk, jnp.max(c)

            first_start = (step * tiles + g + 1) * tq + pad - tk
            lax.while_loop(more, body, (first_start, jnp.max(c_ref[g])))
            o_ref[pl.ds(q0, tq), :] = acc_ref[g].astype(o_ref.dtype)
            return 0

        lax.fori_loop(0, tiles, walk, 0)


def _sb_attention(qkv, *, d, pad, tiles):
    bsz, s_pad, _ = qkv.shape
    s = s_pad - pad
    pairs = d // LANES
    rows = tiles * Q_TILE
    assert pad % rows == 0 and s % rows == 0
    tq, tk = Q_TILE, KEY_BLOCK
    kr = lax.broadcasted_iota(jnp.int32, (tk, tk), 0)
    kc = lax.broadcasted_iota(jnp.int32, (tk, tk), 1)
    tri = jnp.where(kr > kc, -1.0, 0.0).astype(BF16)
    assert tq <= LANES and tk % LANES == 0
    mr = lax.broadcasted_iota(jnp.int32, (2 * tq, LANES), 0) & (tq - 1)
    mc = lax.broadcasted_iota(jnp.int32, (2 * tq, LANES), 1)
    bias = jnp.where(mc - (LANES - tq) < mr, 0.0, MASKED).astype(F32)
    const = lambda b, p, i: (0, 0)
    return pl.pallas_call(
        functools.partial(_sb_attn_kernel, tiles=tiles, pad=pad),
        grid=(bsz, pairs, s // rows),
        in_specs=[
            pl.BlockSpec((None, rows, LANES), lambda b, p, i: (b, i + pad // rows, p)),
            pl.BlockSpec((None, s_pad, LANES), lambda b, p, i: (b, 0, pairs + p)),
            pl.BlockSpec((None, s_pad, LANES), lambda b, p, i: (b, 0, 2 * pairs + p)),
            pl.BlockSpec((tk, tk), const),
            pl.BlockSpec((2 * tq, LANES), const),
        ],
        out_specs=pl.BlockSpec((None, rows, LANES), lambda b, p, i: (b, i, p)),
        out_shape=jax.ShapeDtypeStruct((bsz, s, d), BF16),
        scratch_shapes=[
            pltpu.VMEM((tiles, Q_TILE, LANES), F32),
            pltpu.VMEM((tiles, 2 * Q_TILE, 1), F32),
        ],
        compiler_params=pltpu.CompilerParams(
            dimension_semantics=("arbitrary", "arbitrary", "arbitrary"),
            vmem_limit_bytes=VMEM_LIMIT),
        name="sb_attention",
    )(qkv, qkv, qkv, tri, bias)


def kernel(x, norm_mix_g, norm_ffn_g, a_w_in, a_conv_w, a_conv_b, a_w_r, a_b_r, a_w_i, a_b_i,
           a_lambda, a_w_out, b_w_qkv, b_w_out, ffn_w_gate, ffn_w_up, ffn_w_down, final_g):
    bsz, s, d = x.shape
    m = bsz * s
    row = lambda v: v.reshape(1, -1)
    bf = lambda w: w.astype(BF16)

    x = _rglru_mixer(x, row(norm_mix_g[0]), bf(a_w_in[0]), a_conv_w[0], row(a_conv_b[0]),
                     bf(a_w_r[0]), row(a_b_r[0]), bf(a_w_i[0]), row(a_b_i[0]),
                     row(a_lambda[0]), bf(a_w_out[0]), tm=512)
    x = _ffn(x.reshape(m, d), row(norm_ffn_g[0]), ffn_w_gate, ffn_w_up, ffn_w_down, 0,
             tm=1024, tf=256)

    tiles = 32
    pad = tiles * Q_TILE
    qkv = _qkv_proj(x.reshape(bsz, s, d), row(norm_mix_g[1]), b_w_qkv[0], tm=1024, pad=pad)
    att = _sb_attention(qkv, d=d, pad=pad, tiles=tiles)
    out = _ffn(x, row(norm_ffn_g[1]), ffn_w_gate, ffn_w_up, ffn_w_down, 1,
               att=att.reshape(m, d), w_o=b_w_out[0], final_g=row(final_g), tm=512, tf=256)
    return out.reshape(bsz, s, d)
```

```python
import functools
import math

import jax
import jax.numpy as jnp
from jax import lax
from jax.experimental import pallas as pl
from jax.experimental.pallas import tpu as pltpu

RMS_EPS = 1e-6
RG_C = 8.0
CONV_W = 4
RG_BLOCKS = 4
SB_HEAD_DIM = 64
LANES = 128
SUBLANES = 8
ROW_CHUNK = 256
Q_TILE = 64
KEY_BLOCK = 256
LOG2E = 1.4426950408889634
GELU_C0 = math.sqrt(2.0 / math.pi)
EXIT_LOG = -130.0
MASKED = -1e30
VMEM_LIMIT = 56 * 1024 * 1024

BF16 = jnp.bfloat16
F32 = jnp.float32


def _rmsnorm(x, g):
    ms = jnp.mean(x * x, axis=-1, keepdims=True)
    return x * lax.rsqrt(ms + RMS_EPS) * g


def _dot(a, b):
    return jnp.dot(a, b, preferred_element_type=F32)


def _gelu_tanh(x):
    inner = x * (GELU_C0 + (GELU_C0 * 0.044715) * (x * x))
    hx = 0.5 * x
    return hx + hx * jnp.tanh(inner)


def _softplus(z):
    return jnp.maximum(z, 0.0) + jnp.log(1.0 + jnp.exp(-jnp.abs(z)))


def _rglru_kernel(x_ref, g_ref, win_ref, cw_ref, cb_ref, wr_ref, br_ref, wi_ref, bi_ref,
                  lam_ref, wout_ref, o_ref, xpad_ref, a_ref, u_ref, hs_ref, h_ref):
    tm, d = x_ref.shape
    bw = d // RG_BLOCKS
    pad = SUBLANES

    @pl.when(pl.program_id(1) == 0)
    def _():
        xpad_ref[0:pad, :] = jnp.zeros((pad, d), F32)
        h_ref[...] = jnp.zeros_like(h_ref)

    decay_rate = RG_C * -_softplus(-lam_ref[...])

    gates = []
    h = h_ref[...]
    for c0 in range(0, tm, ROW_CHUNK):
        rows = slice(c0, c0 + ROW_CHUNK)
        hb = _rmsnorm(x_ref[rows, :], g_ref[...]).astype(BF16)
        xpad_ref[pad + c0:pad + c0 + ROW_CHUNK, :] = _dot(hb, win_ref[:, d:])

        xc = cb_ref[...] + cw_ref[CONV_W - 1:CONV_W, :] * xpad_ref[pad + c0:pad + c0 + ROW_CHUNK, :]
        for k in range(CONV_W - 1):
            off = pad + c0 - (CONV_W - 1) + k
            xc = xc + cw_ref[k:k + 1, :] * xpad_ref[off:off + ROW_CHUNK, :]

        xcb = xc.astype(BF16)
        for n in range(RG_BLOCKS):
            sl = slice(n * bw, (n + 1) * bw)
            r = jax.nn.sigmoid(_dot(xcb[:, sl], wr_ref[n]) + br_ref[:, sl])
            i = jax.nn.sigmoid(_dot(xcb[:, sl], wi_ref[n]) + bi_ref[:, sl])
            log_a = r * decay_rate[:, sl]
            a = jnp.exp(log_a)
            mult = jnp.exp2(jnp.log(jnp.maximum(jnp.tanh(log_a) * (-1.0 - a * a), 0.0))
                            * (0.5 * LOG2E))
            a_ref[rows, sl] = a
            u_ref[rows, sl] = mult * (i * xc[:, sl])

        for t in range(c0, c0 + ROW_CHUNK):
            h = a_ref[t:t + 1, :] * h + u_ref[t:t + 1, :]
            hs_ref[t:t + 1, :] = h

        gates.append(_gelu_tanh(_dot(hb, win_ref[:, :d])))
    h_ref[...] = h
    xpad_ref[0:pad, :] = xpad_ref[tm:tm + pad, :]

    for c, c0 in enumerate(range(0, tm, ROW_CHUNK)):
        rows = slice(c0, c0 + ROW_CHUNK)
        y = (hs_ref[rows, :] * gates[c]).astype(BF16)
        o_ref[rows, :] = x_ref[rows, :] + _dot(y, wout_ref[...])


def _rglru_mixer(x, g, w_in, conv_w, conv_b, w_r, b_r, w_i, b_i, lam, w_out, *, tm):
    bsz, s, d = x.shape
    bw = d // RG_BLOCKS
    const2 = lambda b, i: (0, 0)
    const3 = lambda b, i: (0, 0, 0)
    vec = pl.BlockSpec((1, d), const2)
    return pl.pallas_call(
        _rglru_kernel,
        grid=(bsz, s // tm),
        in_specs=[
            pl.BlockSpec((None, tm, d), lambda b, i: (b, i, 0)),
            vec,
            pl.BlockSpec((d, 2 * d), const2),
            pl.BlockSpec((CONV_W, d), const2),
            vec,
            pl.BlockSpec((RG_BLOCKS, bw, bw), const3),
            vec,
            pl.BlockSpec((RG_BLOCKS, bw, bw), const3),
            vec,
            vec,
            pl.BlockSpec((d, d), const2),
        ],
        out_specs=pl.BlockSpec((None, tm, d), lambda b, i: (b, i, 0)),
        out_shape=jax.ShapeDtypeStruct((bsz, s, d), F32),
        scratch_shapes=[
            pltpu.VMEM((tm + SUBLANES, d), F32),
            pltpu.VMEM((tm, d), F32),
            pltpu.VMEM((tm, d), F32),
            pltpu.VMEM((tm, d), F32),
            pltpu.VMEM((1, d), F32),
        ],
        compiler_params=pltpu.CompilerParams(
            dimension_semantics=("arbitrary", "arbitrary"),
            vmem_limit_bytes=VMEM_LIMIT),
        name="rglru_mixer",
    )(x, g, w_in, conv_w, conv_b, w_r, b_r, w_i, b_i, lam, w_out)


def _ffn_kernel(*refs, pre_proj, final_norm, nchunk):
    refs = list(refs)
    x_ref = refs.pop(0)
    if pre_proj:
        att_ref = refs.pop(0)
        wo_ref = refs.pop(0)
    g_ref, wg_ref, wu_ref, wd_ref = refs[:4]
    refs = refs[4:]
    if final_norm:
        fg_ref = refs.pop(0)
    o_ref, wg_s, wu_s, wd_s, hb_s = refs

    t = pl.program_id(0)

    def start_tile():
        x = x_ref[...]
        if pre_proj:
            x = x + _dot(att_ref[...], wo_ref[...].astype(BF16))
        o_ref[...] = x
        return _rmsnorm(x, g_ref[...]).astype(BF16)

    def add_chunk(h, c):
        gate = _dot(h, wg_s[c])
        up = _dot(h, wu_s[c])
        act = (gate * jax.nn.sigmoid(gate) * up).astype(BF16)
        o_ref[...] += _dot(act, wd_s[c])

    def finish_tile():
        if final_norm:
            o_ref[...] = _rmsnorm(o_ref[...], fg_ref[...])

    @pl.when(t < nchunk)
    def _():
        wg_s[t] = wg_ref[...].astype(BF16)
        wu_s[t] = wu_ref[...].astype(BF16)
        wd_s[t] = wd_ref[...].astype(BF16)

        @pl.when(t == 0)
        def _():
            hb_s[...] = start_tile()

        add_chunk(hb_s[...], t)

        @pl.when(t == nchunk - 1)
        def _():
            finish_tile()

    @pl.when(t >= nchunk)
    def _():
        h = start_tile()
        for c in range(nchunk):
            add_chunk(h, c)
        finish_tile()


def _ffn(x, g, w_gate, w_up, w_down, layer, *, att=None, w_o=None, final_g=None, tm, tf):
    m, d = x.shape
    dff = w_gate.shape[2]
    assert dff % tf == 0 and m % tm == 0
    nchunk = dff // tf
    pre_proj = att is not None
    final_norm = final_g is not None
    resident = dict(pipeline_mode=pl.Buffered(1))
    tile_of = lambda t: jnp.maximum(t - (nchunk - 1), 0)
    chunk_of = lambda t: jnp.minimum(t, nchunk - 1)
    row = pl.BlockSpec((tm, d), lambda t: (tile_of(t), 0))
    vec = pl.BlockSpec((1, d), lambda t: (0, 0), **resident)
    args, specs = [x], [row]
    if pre_proj:
        args += [att, w_o]
        specs += [row, pl.BlockSpec((d, d), lambda t: (0, 0), **resident)]
    args += [g, w_gate, w_up, w_down]
    specs += [vec,
              pl.BlockSpec((None, d, tf), lambda t: (layer, 0, chunk_of(t))),
              pl.BlockSpec((None, d, tf), lambda t: (layer, 0, chunk_of(t))),
              pl.BlockSpec((None, tf, d), lambda t: (layer, chunk_of(t), 0))]
    if final_norm:
        args.append(final_g)
        specs.append(vec)
    return pl.pallas_call(
        functools.partial(_ffn_kernel, pre_proj=pre_proj, final_norm=final_norm, nchunk=nchunk),
        grid=(nchunk + m // tm - 1,),
        in_specs=specs,
        out_specs=row,
        out_shape=jax.ShapeDtypeStruct((m, d), F32),
        scratch_shapes=[
            pltpu.VMEM((nchunk, d, tf), BF16),
            pltpu.VMEM((nchunk, d, tf), BF16),
            pltpu.VMEM((nchunk, tf, d), BF16),
            pltpu.VMEM((tm, d), BF16),
        ],
        compiler_params=pltpu.CompilerParams(
            dimension_semantics=("arbitrary",),
            vmem_limit_bytes=VMEM_LIMIT),
        name="ffn_attn_out" if pre_proj else "ffn",
    )(*args)


def _qkv_kernel(x_ref, g_ref, w_ref, o_ref, w_s, *, q_scale, nload, pad_tiles, tiles_per_seq):
    d = x_ref.shape[1]
    t = pl.program_id(0)

    @pl.when(t < nload)
    def _():
        w_s[t] = w_ref[...].astype(BF16)

    item = jnp.maximum(t - (nload - pad_tiles), 0)
    is_pad = lax.rem(item, tiles_per_seq) < pad_tiles

    @pl.when(is_pad)
    def _():
        o_ref[...] = jnp.zeros_like(o_ref)

    @pl.when(jnp.logical_not(is_pad))
    def _():
        h = _rmsnorm(x_ref[...], g_ref[...]).astype(BF16)
        for c in range(nload):
            y = _dot(h, w_s[c])
            if c == 0:
                y = y * q_scale
            o_ref[:, c * d:(c + 1) * d] = y.astype(o_ref.dtype)


def _qkv_proj(x, g, w_qkv, *, tm, pad):
    bsz, s, d = x.shape
    n = w_qkv.shape[1]
    nload = n // d
    pad_tiles = pad // tm
    assert pad % tm == 0 and s % tm == 0 and 1 <= pad_tiles <= nload
    tiles_per_seq = s // tm + pad_tiles
    resident = dict(pipeline_mode=pl.Buffered(1))
    item = lambda t: jnp.maximum(t - (nload - pad_tiles), 0)
    seq_of = lambda t: item(t) // tiles_per_seq
    tile_of = lambda t: item(t) % tiles_per_seq
    return pl.pallas_call(
        functools.partial(_qkv_kernel, q_scale=LOG2E / math.sqrt(SB_HEAD_DIM), nload=nload,
                          pad_tiles=pad_tiles, tiles_per_seq=tiles_per_seq),
        grid=(nload - pad_tiles + bsz * tiles_per_seq,),
        in_specs=[
            pl.BlockSpec((None, tm, d),
                         lambda t: (seq_of(t), jnp.maximum(tile_of(t) - pad_tiles, 0), 0)),
            pl.BlockSpec((1, d), lambda t: (0, 0), **resident),
            pl.BlockSpec((d, d), lambda t: (0, jnp.minimum(t, nload - 1))),
        ],
        out_specs=pl.BlockSpec((None, tm, n), lambda t: (seq_of(t), tile_of(t), 0)),
        out_shape=jax.ShapeDtypeStruct((bsz, s + pad, n), BF16),
        scratch_shapes=[pltpu.VMEM((nload, d, d), BF16)],
        compiler_params=pltpu.CompilerParams(
            dimension_semantics=("arbitrary",),
            vmem_limit_bytes=VMEM_LIMIT),
        name="qkv_proj",
    )(x, g, w_qkv)


def _sb_attn_kernel(q_ref, k_ref, v_ref, tri_ref, bias_ref, o_ref, acc_ref, c_ref, *, tiles, pad):
    tq, tk = Q_TILE, KEY_BLOCK
    step = pl.program_id(2)

    lane = lax.broadcasted_iota(jnp.int32, (tq, LANES), 1)
    head0, head1 = lane < SB_HEAD_DIM, lane >= SB_HEAD_DIM

    def scores(q, r0, first):
        zq = jnp.zeros_like(q)
        qs = jnp.concatenate([jnp.where(head0, q, zq), jnp.where(head1, q, zq)], axis=0)
        k = k_ref[pl.ds(r0, tk), :]
        z = lax.dot_general(qs, k, (((1,), (1,)), ((), ())), preferred_element_type=F32)
        if first:
            z = jnp.concatenate([z[:, :tk - LANES], z[:, tk - LANES:] + bias_ref[...]], axis=1)
        return z

    def weights(z):
        zneg, zpos = jnp.minimum(z, 0.0), jnp.maximum(z, 0.0)
        lg = jnp.log(1.0 + jnp.exp2(zneg - zpos)) * LOG2E
        sp = zpos + lg
        sfx = _dot(sp.astype(BF16), tri_ref[...])
        w = jnp.exp2((zneg - lg) + sfx)
        total = sfx[:, 0:1] - sp[:, 0:1]
        return w.astype(BF16), total

    def weighted_values(w, r0):
        return _dot(w, v_ref[pl.ds(r0, tk), :])

    rows2 = 2 * tq
    ends = [pl.multiple_of((step * tiles + g + 1) * tq + pad, tq) for g in range(tiles)]
    w, total = weights(jnp.concatenate(
        [scores(q_ref[g * tq:(g + 1) * tq, :], ends[g] - tk, True) for g in range(tiles)],
        axis=0))
    for g in range(tiles):
        pv = weighted_values(w[g * rows2:(g + 1) * rows2], ends[g] - tk)
        acc = jnp.where(head0, pv[:tq], pv[tq:])
        o_ref[g * tq:(g + 1) * tq, :] = acc.astype(o_ref.dtype)
        acc_ref[g] = acc
        c_ref[g] = total[g * rows2:(g + 1) * rows2]

    @pl.when(jnp.max(total) > EXIT_LOG)
    def _():
        def more(carry):
            end, cm = carry
            return (end > pad) & (cm > EXIT_LOG)

        def walk(g, _):
            q0 = pl.multiple_of(g * tq, tq)
            q = q_ref[pl.ds(q0, tq), :]

            def body(carry):
                end, _ = carry
                r0 = pl.multiple_of(end - tk, tq)
                w, total = weights(scores(q, r0, False))
                c = c_ref[g]
                pv = jnp.exp2(c) * weighted_values(w, r0)
                acc_ref[g] += jnp.where(head0, pv[:tq], pv[tq:])
                c = c + total
                c_ref[g] = c
                return end - tk, jnp.max(c)

            first_start = (step * tiles + g + 1) * tq + pad - tk
            lax.while_loop(more, body, (first_start, jnp.max(c_ref[g])))
            o_ref[pl.ds(q0, tq), :] = acc_ref[g].astype(o_ref.dtype)
            return 0

        lax.fori_loop(0, tiles, walk, 0)


def _sb_attention(qkv, *, d, pad, tiles):
    bsz, s_pad, _ = qkv.shape
    s = s_pad - pad
    pairs = d // LANES
    rows = tiles * Q_TILE
    assert pad % rows == 0 and s % rows == 0
    tq, tk = Q_TILE, KEY_BLOCK
    kr = lax.broadcasted_iota(jnp.int32, (tk, tk), 0)
    kc = lax.broadcasted_iota(jnp.int32, (tk, tk), 1)
    tri = jnp.where(kr > kc, -1.0, 0.0).astype(BF16)
    assert tq <= LANES and tk % LANES == 0
    mr = lax.broadcasted_iota(jnp.int32, (2 * tq, LANES), 0) & (tq - 1)
    mc = lax.broadcasted_iota(jnp.int32, (2 * tq, LANES), 1)
    bias = jnp.where(mc - (LANES - tq) < mr, 0.0, MASKED).astype(F32)
    const = lambda b, p, i: (0, 0)
    return pl.pallas_call(
        functools.partial(_sb_attn_kernel, tiles=tiles, pad=pad),
        grid=(bsz, pairs, s // rows),
        in_specs=[
            pl.BlockSpec((None, rows, LANES), lambda b, p, i: (b, i + pad // rows, p)),
            pl.BlockSpec((None, s_pad, LANES), lambda b, p, i: (b, 0, pairs + p)),
            pl.BlockSpec((None, s_pad, LANES), lambda b, p, i: (b, 0, 2 * pairs + p)),
            pl.BlockSpec((tk, tk), const),
            pl.BlockSpec((2 * tq, LANES), const),
        ],
        out_specs=pl.BlockSpec((None, rows, LANES), lambda b, p, i: (b, i, p)),
        out_shape=jax.ShapeDtypeStruct((bsz, s, d), BF16),
        scratch_shapes=[
            pltpu.VMEM((tiles, Q_TILE, LANES), F32),
            pltpu.VMEM((tiles, 2 * Q_TILE, 1), F32),
        ],
        compiler_params=pltpu.CompilerParams(
            dimension_semantics=("arbitrary", "arbitrary", "arbitrary"),
            vmem_limit_bytes=VMEM_LIMIT),
        name="sb_attention",
    )(qkv, qkv, qkv, tri, bias)


def kernel(x, norm_mix_g, norm_ffn_g, a_w_in, a_conv_w, a_conv_b, a_w_r, a_b_r, a_w_i, a_b_i,
           a_lambda, a_w_out, b_w_qkv, b_w_out, ffn_w_gate, ffn_w_up, ffn_w_down, final_g):
    bsz, s, d = x.shape
    m = bsz * s
    row = lambda v: v.reshape(1, -1)
    bf = lambda w: w.astype(BF16)

    x = _rglru_mixer(x, row(norm_mix_g[0]), bf(a_w_in[0]), a_conv_w[0], row(a_conv_b[0]),
                     bf(a_w_r[0]), row(a_b_r[0]), bf(a_w_i[0]), row(a_b_i[0]),
                     row(a_lambda[0]), bf(a_w_out[0]), tm=512)
    x = _ffn(x.reshape(m, d), row(norm_ffn_g[0]), ffn_w_gate, ffn_w_up, ffn_w_down, 0,
             tm=1024, tf=256)

    tiles = 32
    pad = tiles * Q_TILE
    qkv = _qkv_proj(x.reshape(bsz, s, d), row(norm_mix_g[1]), b_w_qkv[0], tm=1024, pad=pad)
    att = _sb_attention(qkv, d=d, pad=pad, tiles=tiles)
    out = _ffn(x, row(norm_ffn_g[1]), ffn_w_gate, ffn_w_up, ffn_w_down, 1,
               att=att.reshape(m, d), w_o=b_w_out[0], final_g=row(final_g), tm=1024, tf=256)
    return out.reshape(bsz, s, d)
```

```python
import functools
import math

import jax
import jax.numpy as jnp
from jax import lax
from jax.experimental import pallas as pl
from jax.experimental.pallas import tpu as pltpu

RMS_EPS = 1e-6
RG_C = 8.0
CONV_W = 4
RG_BLOCKS = 4
SB_HEAD_DIM = 64
LANES = 128
SUBLANES = 8
ROW_CHUNK = 256
Q_TILE = 64
KEY_BLOCK = 256
LOG2E = 1.4426950408889634
GELU_C0 = math.sqrt(2.0 / math.pi)
EXIT_LOG = -130.0
MASKED = -1e30
VMEM_LIMIT = 56 * 1024 * 1024

BF16 = jnp.bfloat16
F32 = jnp.float32


def _rmsnorm(x, g):
    ms = jnp.mean(x * x, axis=-1, keepdims=True)
    return x * lax.rsqrt(ms + RMS_EPS) * g


def _dot(a, b):
    return jnp.dot(a, b, preferred_element_type=F32)


def _gelu_tanh(x):
    inner = x * (GELU_C0 + (GELU_C0 * 0.044715) * (x * x))
    hx = 0.5 * x
    return hx + hx * jnp.tanh(inner)


def _softplus(z):
    return jnp.maximum(z, 0.0) + jnp.log(1.0 + jnp.exp(-jnp.abs(z)))


def _rglru_kernel(x_ref, g_ref, win_ref, cw_ref, cb_ref, wr_ref, br_ref, wi_ref, bi_ref,
                  lam_ref, wout_ref, o_ref, xpad_ref, a_ref, u_ref, hs_ref, h_ref):
    tm, d = x_ref.shape
    bw = d // RG_BLOCKS
    pad = SUBLANES

    @pl.when(pl.program_id(1) == 0)
    def _():
        xpad_ref[0:pad, :] = jnp.zeros((pad, d), F32)
        h_ref[...] = jnp.zeros_like(h_ref)

    decay_rate = RG_C * -_softplus(-lam_ref[...])

    gates = []
    h = h_ref[...]
    for c0 in range(0, tm, ROW_CHUNK):
        rows = slice(c0, c0 + ROW_CHUNK)
        hb = _rmsnorm(x_ref[rows, :], g_ref[...]).astype(BF16)
        xpad_ref[pad + c0:pad + c0 + ROW_CHUNK, :] = _dot(hb, win_ref[:, d:])

        xc = cb_ref[...] + cw_ref[CONV_W - 1:CONV_W, :] * xpad_ref[pad + c0:pad + c0 + ROW_CHUNK, :]
        for k in range(CONV_W - 1):
            off = pad + c0 - (CONV_W - 1) + k
            xc = xc + cw_ref[k:k + 1, :] * xpad_ref[off:off + ROW_CHUNK, :]

        xcb = xc.astype(BF16)
        for n in range(RG_BLOCKS):
            sl = slice(n * bw, (n + 1) * bw)
            r = jax.nn.sigmoid(_dot(xcb[:, sl], wr_ref[n]) + br_ref[:, sl])
            i = jax.nn.sigmoid(_dot(xcb[:, sl], wi_ref[n]) + bi_ref[:, sl])
            log_a = r * decay_rate[:, sl]
            a = jnp.exp(log_a)
            mult = jnp.exp2(jnp.log(jnp.maximum(jnp.tanh(log_a) * (-1.0 - a * a), 0.0))
                            * (0.5 * LOG2E))
            a_ref[rows, sl] = a
            u_ref[rows, sl] = mult * (i * xc[:, sl])

        for t in range(c0, c0 + ROW_CHUNK):
            h = a_ref[t:t + 1, :] * h + u_ref[t:t + 1, :]
            hs_ref[t:t + 1, :] = h

        gates.append(_gelu_tanh(_dot(hb, win_ref[:, :d])))
    h_ref[...] = h
    xpad_ref[0:pad, :] = xpad_ref[tm:tm + pad, :]

    for c, c0 in enumerate(range(0, tm, ROW_CHUNK)):
        rows = slice(c0, c0 + ROW_CHUNK)
        y = (hs_ref[rows, :] * gates[c]).astype(BF16)
        o_ref[rows, :] = x_ref[rows, :] + _dot(y, wout_ref[...])


def _rglru_mixer(x, g, w_in, conv_w, conv_b, w_r, b_r, w_i, b_i, lam, w_out, *, tm):
    bsz, s, d = x.shape
    bw = d // RG_BLOCKS
    const2 = lambda b, i: (0, 0)
    const3 = lambda b, i: (0, 0, 0)
    vec = pl.BlockSpec((1, d), const2)
    return pl.pallas_call(
        _rglru_kernel,
        grid=(bsz, s // tm),
        in_specs=[
            pl.BlockSpec((None, tm, d), lambda b, i: (b, i, 0)),
            vec,
            pl.BlockSpec((d, 2 * d), const2),
            pl.BlockSpec((CONV_W, d), const2),
            vec,
            pl.BlockSpec((RG_BLOCKS, bw, bw), const3),
            vec,
            pl.BlockSpec((RG_BLOCKS, bw, bw), const3),
            vec,
            vec,
            pl.BlockSpec((d, d), const2),
        ],
        out_specs=pl.BlockSpec((None, tm, d), lambda b, i: (b, i, 0)),
        out_shape=jax.ShapeDtypeStruct((bsz, s, d), F32),
        scratch_shapes=[
            pltpu.VMEM((tm + SUBLANES, d), F32),
            pltpu.VMEM((tm, d), F32),
            pltpu.VMEM((tm, d), F32),
            pltpu.VMEM((tm, d), F32),
            pltpu.VMEM((1, d), F32),
        ],
        compiler_params=pltpu.CompilerParams(
            dimension_semantics=("arbitrary", "arbitrary"),
            vmem_limit_bytes=VMEM_LIMIT),
        name="rglru_mixer",
    )(x, g, w_in, conv_w, conv_b, w_r, b_r, w_i, b_i, lam, w_out)


def _layer0_kernel(x_ref, gm_ref, win_ref, cw_ref, cb_ref, wr_ref, br_ref, wi_ref, bi_ref,
                   lam_ref, wout_ref, gf_ref, wg_ref, wu_ref, wd_ref, o_ref,
                   xpad_ref, a_ref, u_ref, hs_ref, h_ref, x1_ref, wg_s, wu_s, wd_s, *,
                   nchunk, tiles_per_seq, ntiles):
    tm, d = x_ref.shape
    bw = d // RG_BLOCKS
    pad = SUBLANES
    t = pl.program_id(0)

    @pl.when(t == 0)
    def _():
        x1_ref[...] = jnp.zeros_like(x1_ref)

    @pl.when(t < nchunk)
    def _():
        wg_s[t] = wg_ref[...].astype(BF16)
        wu_s[t] = wu_ref[...].astype(BF16)
        wd_s[t] = wd_ref[...].astype(BF16)

    @pl.when(t >= nchunk)
    def _():
        r = t - nchunk
        slot = lax.rem(r, 2)

        @pl.when(lax.rem(jnp.minimum(r, ntiles - 1), tiles_per_seq) == 0)
        def _():
            xpad_ref[0:pad, :] = jnp.zeros((pad, d), F32)
            h_ref[...] = jnp.zeros_like(h_ref)

        decay_rate = RG_C * -_softplus(-lam_ref[...])
        hb, xc, xcb, pre, gates = {}, {}, {}, {}, {}

        x1 = x1_ref[1 - slot]
        o_ref[...] = x1
        hf = _rmsnorm(x1, gf_ref[...]).astype(BF16)

        acts = {}

        def ffn_unit(k):
            if k < nchunk:
                gate = _dot(hf, wg_s[k])
                up = _dot(hf, wu_s[k])
                acts[k] = (gate * jax.nn.sigmoid(gate) * up).astype(BF16)
            if k >= 1:
                o_ref[...] += _dot(acts.pop(k - 1), wd_s[k - 1])

        def x_branch(c0):
            hb[c0] = _rmsnorm(x_ref[c0:c0 + ROW_CHUNK, :], gm_ref[...]).astype(BF16)
            xpad_ref[pad + c0:pad + c0 + ROW_CHUNK, :] = _dot(hb[c0], win_ref[:, d:])

        def conv(c0):
            y = cb_ref[...] + (cw_ref[CONV_W - 1:CONV_W, :]
                               * xpad_ref[pad + c0:pad + c0 + ROW_CHUNK, :])
            for k in range(CONV_W - 1):
                off = pad + c0 - (CONV_W - 1) + k
                y = y + cw_ref[k:k + 1, :] * xpad_ref[off:off + ROW_CHUNK, :]
            xc[c0] = y
            xcb[c0] = y.astype(BF16)

        def gate_dots(c0):
            pre[c0] = [(_dot(xcb[c0][:, n * bw:(n + 1) * bw], wr_ref[n]),
                        _dot(xcb[c0][:, n * bw:(n + 1) * bw], wi_ref[n]))
                       for n in range(RG_BLOCKS)]

        def gate_block(c0, n):
            rows = slice(c0, c0 + ROW_CHUNK)
            sl = slice(n * bw, (n + 1) * bw)
            rg = jax.nn.sigmoid(pre[c0][n][0] + br_ref[:, sl])
            ig = jax.nn.sigmoid(pre[c0][n][1] + bi_ref[:, sl])
            log_a = rg * decay_rate[:, sl]
            a = jnp.exp(log_a)
            mult = jnp.exp2(jnp.log(jnp.maximum(jnp.tanh(log_a) * (-1.0 - a * a), 0.0))
                            * (0.5 * LOG2E))
            a_ref[rows, sl] = a
            u_ref[rows, sl] = mult * (ig * xc[c0][:, sl])

        def recurrence(c0, h):
            for tt in range(c0, c0 + ROW_CHUNK):
                h = a_ref[tt:tt + 1, :] * h + u_ref[tt:tt + 1, :]
                hs_ref[tt:tt + 1, :] = h
            return h

        def gate_branch(c0):
            gates[c0] = _gelu_tanh(_dot(hb[c0], win_ref[:, :d]))

        def out_proj(c0):
            rows = slice(c0, c0 + ROW_CHUNK)
            y = (hs_ref[rows, :] * gates[c0]).astype(BF16)
            x1_ref[slot, rows, :] = x_ref[rows, :] + _dot(y, wout_ref[...])

        units = list(range(nchunk + 1))

        def ffn_units(n):
            for _ in range(n):
                if units:
                    ffn_unit(units.pop(0))

        h = h_ref[...]
        starts = list(range(0, tm, ROW_CHUNK))
        for c0 in starts:
            x_branch(c0)
            ffn_units(1)
            conv(c0)
            gate_dots(c0)
            for n in range(RG_BLOCKS):
                gate_block(c0, n)
                ffn_units(1)
            h = recurrence(c0, h)
            gate_branch(c0)
            ffn_units(1)
        h_ref[...] = h
        xpad_ref[0:pad, :] = xpad_ref[tm:tm + pad, :]
        for c0 in starts:
            out_proj(c0)
        ffn_units(len(units))


def _rglru_ffn_layer(x, g_mix, w_in, conv_w, conv_b, w_r, b_r, w_i, b_i, lam, w_out,
                     g_ffn, w_gate, w_up, w_down, layer, *, tm, tf):
    bsz, s, d = x.shape
    bw = d // RG_BLOCKS
    dff = w_gate.shape[2]
    assert s % tm == 0 and tm % ROW_CHUNK == 0 and dff % tf == 0
    nchunk = dff // tf
    tiles_per_seq = s // tm
    ntiles = bsz * tiles_per_seq
    resident = dict(pipeline_mode=pl.Buffered(1))
    mixer_tile = lambda t: jnp.clip(t - nchunk, 0, ntiles - 1)
    ffn_tile = lambda t: jnp.clip(t - nchunk - 1, 0, ntiles - 1)
    chunk_of = lambda t: jnp.minimum(t, nchunk - 1)
    vec = pl.BlockSpec((1, d), lambda t: (0, 0), **resident)
    return pl.pallas_call(
        functools.partial(_layer0_kernel, nchunk=nchunk, tiles_per_seq=tiles_per_seq,
                          ntiles=ntiles),
        grid=(nchunk + ntiles + 1,),
        in_specs=[
            pl.BlockSpec((None, tm, d), lambda t: (mixer_tile(t) // tiles_per_seq,
                                                   mixer_tile(t) % tiles_per_seq, 0)),
            vec,
            pl.BlockSpec((d, 2 * d), lambda t: (0, 0), **resident),
            pl.BlockSpec((CONV_W, d), lambda t: (0, 0), **resident),
            vec,
            pl.BlockSpec((RG_BLOCKS, bw, bw), lambda t: (0, 0, 0), **resident),
            vec,
            pl.BlockSpec((RG_BLOCKS, bw, bw), lambda t: (0, 0, 0), **resident),
            vec,
            vec,
            pl.BlockSpec((d, d), lambda t: (0, 0), **resident),
            vec,
            pl.BlockSpec((None, d, tf), lambda t: (layer, 0, chunk_of(t)), **resident),
            pl.BlockSpec((None, d, tf), lambda t: (layer, 0, chunk_of(t)), **resident),
            pl.BlockSpec((None, tf, d), lambda t: (layer, chunk_of(t), 0), **resident),
        ],
        out_specs=pl.BlockSpec((tm, d), lambda t: (ffn_tile(t), 0)),
        out_shape=jax.ShapeDtypeStruct((bsz * s, d), F32),
        scratch_shapes=[
            pltpu.VMEM((tm + SUBLANES, d), F32),
            pltpu.VMEM((tm, d), F32),
            pltpu.VMEM((tm, d), F32),
            pltpu.VMEM((tm, d), F32),
            pltpu.VMEM((1, d), F32),
            pltpu.VMEM((2, tm, d), F32),
            pltpu.VMEM((nchunk, d, tf), BF16),
            pltpu.VMEM((nchunk, d, tf), BF16),
            pltpu.VMEM((nchunk, tf, d), BF16),
        ],
        compiler_params=pltpu.CompilerParams(
            dimension_semantics=("arbitrary",),
            vmem_limit_bytes=VMEM_LIMIT),
        name="rglru_ffn",
    )(x, g_mix, w_in, conv_w, conv_b, w_r, b_r, w_i, b_i, lam, w_out, g_ffn, w_gate, w_up, w_down)


def _ffn_kernel(*refs, pre_proj, final_norm, nchunk):
    refs = list(refs)
    x_ref = refs.pop(0)
    if pre_proj:
        att_ref = refs.pop(0)
        wo_ref = refs.pop(0)
    g_ref, wg_ref, wu_ref, wd_ref = refs[:4]
    refs = refs[4:]
    if final_norm:
        fg_ref = refs.pop(0)
    o_ref, wg_s, wu_s, wd_s, hb_s = refs

    t = pl.program_id(0)

    def start_tile():
        x = x_ref[...]
        if pre_proj:
            x = x + _dot(att_ref[...], wo_ref[...].astype(BF16))
        o_ref[...] = x
        return _rmsnorm(x, g_ref[...]).astype(BF16)

    def add_chunk(h, c):
        gate = _dot(h, wg_s[c])
        up = _dot(h, wu_s[c])
        act = (gate * jax.nn.sigmoid(gate) * up).astype(BF16)
        o_ref[...] += _dot(act, wd_s[c])

    def finish_tile():
        if final_norm:
            o_ref[...] = _rmsnorm(o_ref[...], fg_ref[...])

    @pl.when(t < nchunk)
    def _():
        wg_s[t] = wg_ref[...].astype(BF16)
        wu_s[t] = wu_ref[...].astype(BF16)
        wd_s[t] = wd_ref[...].astype(BF16)

        @pl.when(t == 0)
        def _():
            hb_s[...] = start_tile()

        add_chunk(hb_s[...], t)

        @pl.when(t == nchunk - 1)
        def _():
            finish_tile()

    @pl.when(t >= nchunk)
    def _():
        h = start_tile()
        for c in range(nchunk):
            add_chunk(h, c)
        finish_tile()


def _ffn(x, g, w_gate, w_up, w_down, layer, *, att=None, w_o=None, final_g=None, tm, tf):
    m, d = x.shape
    dff = w_gate.shape[2]
    assert dff % tf == 0 and m % tm == 0
    nchunk = dff // tf
    pre_proj = att is not None
    final_norm = final_g is not None
    resident = dict(pipeline_mode=pl.Buffered(1))
    tile_of = lambda t: jnp.maximum(t - (nchunk - 1), 0)
    chunk_of = lambda t: jnp.minimum(t, nchunk - 1)
    row = pl.BlockSpec((tm, d), lambda t: (tile_of(t), 0))
    vec = pl.BlockSpec((1, d), lambda t: (0, 0), **resident)
    args, specs = [x], [row]
    if pre_proj:
        args += [att, w_o]
        specs += [row, pl.BlockSpec((d, d), lambda t: (0, 0), **resident)]
    args += [g, w_gate, w_up, w_down]
    specs += [vec,
              pl.BlockSpec((None, d, tf), lambda t: (layer, 0, chunk_of(t))),
              pl.BlockSpec((None, d, tf), lambda t: (layer, 0, chunk_of(t))),
              pl.BlockSpec((None, tf, d), lambda t: (layer, chunk_of(t), 0))]
    if final_norm:
        args.append(final_g)
        specs.append(vec)
    return pl.pallas_call(
        functools.partial(_ffn_kernel, pre_proj=pre_proj, final_norm=final_norm, nchunk=nchunk),
        grid=(nchunk + m // tm - 1,),
        in_specs=specs,
        out_specs=row,
        out_shape=jax.ShapeDtypeStruct((m, d), F32),
        scratch_shapes=[
            pltpu.VMEM((nchunk, d, tf), BF16),
            pltpu.VMEM((nchunk, d, tf), BF16),
            pltpu.VMEM((nchunk, tf, d), BF16),
            pltpu.VMEM((tm, d), BF16),
        ],
        compiler_params=pltpu.CompilerParams(
            dimension_semantics=("arbitrary",),
            vmem_limit_bytes=VMEM_LIMIT),
        name="ffn_attn_out" if pre_proj else "ffn",
    )(*args)


def _qkv_kernel(x_ref, g_ref, w_ref, o_ref, w_s, *, q_scale, nload, pad_tiles, tiles_per_seq):
    d = x_ref.shape[1]
    t = pl.program_id(0)

    @pl.when(t < nload)
    def _():
        w_s[t] = w_ref[...].astype(BF16)

    item = jnp.maximum(t - (nload - pad_tiles), 0)
    is_pad = lax.rem(item, tiles_per_seq) < pad_tiles

    @pl.when(is_pad)
    def _():
        o_ref[...] = jnp.zeros_like(o_ref)

    @pl.when(jnp.logical_not(is_pad))
    def _():
        h = _rmsnorm(x_ref[...], g_ref[...]).astype(BF16)
        for c in range(nload):
            y = _dot(h, w_s[c])
            if c == 0:
                y = y * q_scale
            o_ref[:, c * d:(c + 1) * d] = y.astype(o_ref.dtype)


def _qkv_proj(x, g, w_qkv, *, tm, pad):
    bsz, s, d = x.shape
    n = w_qkv.shape[1]
    nload = n // d
    pad_tiles = pad // tm
    assert pad % tm == 0 and s % tm == 0 and 1 <= pad_tiles <= nload
    tiles_per_seq = s // tm + pad_tiles
    resident = dict(pipeline_mode=pl.Buffered(1))
    item = lambda t: jnp.maximum(t - (nload - pad_tiles), 0)
    seq_of = lambda t: item(t) // tiles_per_seq
    tile_of = lambda t: item(t) % tiles_per_seq
    return pl.pallas_call(
        functools.partial(_qkv_kernel, q_scale=LOG2E / math.sqrt(SB_HEAD_DIM), nload=nload,
                          pad_tiles=pad_tiles, tiles_per_seq=tiles_per_seq),
        grid=(nload - pad_tiles + bsz * tiles_per_seq,),
        in_specs=[
            pl.BlockSpec((None, tm, d),
                         lambda t: (seq_of(t), jnp.maximum(tile_of(t) - pad_tiles, 0), 0)),
            pl.BlockSpec((1, d), lambda t: (0, 0), **resident),
            pl.BlockSpec((d, d), lambda t: (0, jnp.minimum(t, nload - 1))),
        ],
        out_specs=pl.BlockSpec((None, tm, n), lambda t: (seq_of(t), tile_of(t), 0)),
        out_shape=jax.ShapeDtypeStruct((bsz, s + pad, n), BF16),
        scratch_shapes=[pltpu.VMEM((nload, d, d), BF16)],
        compiler_params=pltpu.CompilerParams(
            dimension_semantics=("arbitrary",),
            vmem_limit_bytes=VMEM_LIMIT),
        name="qkv_proj",
    )(x, g, w_qkv)


def _sb_attn_kernel(q_ref, k_ref, v_ref, tri_ref, bias_ref, o_ref, acc_ref, c_ref, *, tiles, pad):
    tq, tk = Q_TILE, KEY_BLOCK
    step = pl.program_id(2)

    lane = lax.broadcasted_iota(jnp.int32, (tq, LANES), 1)
    head0, head1 = lane < SB_HEAD_DIM, lane >= SB_HEAD_DIM

    def scores(q, r0, first):
        zq = jnp.zeros_like(q)
        qs = jnp.concatenate([jnp.where(head0, q, zq), jnp.where(head1, q, zq)], axis=0)
        k = k_ref[pl.ds(r0, tk), :]
        z = lax.dot_general(qs, k, (((1,), (1,)), ((), ())), preferred_element_type=F32)
        if first:
            z = jnp.concatenate([z[:, :tk - LANES], z[:, tk - LANES:] + bias_ref[...]], axis=1)
        return z

    def weights(z):
        zneg, zpos = jnp.minimum(z, 0.0), jnp.maximum(z, 0.0)
        lg = jnp.log(1.0 + jnp.exp2(zneg - zpos)) * LOG2E
        sp = zpos + lg
        sfx = _dot(sp.astype(BF16), tri_ref[...])
        w = jnp.exp2((zneg - lg) + sfx)
        total = sfx[:, 0:1] - sp[:, 0:1]
        return w.astype(BF16), total

    def weighted_values(w, r0):
        return _dot(w, v_ref[pl.ds(r0, tk), :])

    rows2 = 2 * tq
    ends = [pl.multiple_of((step * tiles + g + 1) * tq + pad, tq) for g in range(tiles)]
    w, total = weights(jnp.concatenate(
        [scores(q_ref[g * tq:(g + 1) * tq, :], ends[g] - tk, True) for g in range(tiles)],
        axis=0))
    for g in range(tiles):
        pv = weighted_values(w[g * rows2:(g + 1) * rows2], ends[g] - tk)
        acc = jnp.where(head0, pv[:tq], pv[tq:])
        o_ref[g * tq:(g + 1) * tq, :] = acc.astype(o_ref.dtype)
        acc_ref[g] = acc
        c_ref[g] = total[g * rows2:(g + 1) * rows2]

    @pl.when(jnp.max(total) > EXIT_LOG)
    def _():
        def more(carry):
            end, cm = carry
            return (end > pad) & (cm > EXIT_LOG)

        def walk(g, _):
            q0 = pl.multiple_of(g * tq, tq)
            q = q_ref[pl.ds(q0, tq), :]

            def body(carry):
                end, _ = carry
                r0 = pl.multiple_of(end - tk, tq)
                w, total = weights(scores(q, r0, False))
                c = c_ref[g]
                pv = jnp.exp2(c) * weighted_values(w, r0)
                acc_ref[g] += jnp.where(head0, pv[:tq], pv[tq:])
                c = c + total
                c_ref[g] = c
                return end - tk, jnp.max(c)

            first_start = (step * tiles + g + 1) * tq + pad - tk
            lax.while_loop(more, body, (first_start, jnp.max(c_ref[g])))
            o_ref[pl.ds(q0, tq), :] = acc_ref[g].astype(o_ref.dtype)
            return 0

        lax.fori_loop(0, tiles, walk, 0)


def _sb_attention(qkv, *, d, pad, tiles):
    bsz, s_pad, _ = qkv.shape
    s = s_pad - pad
    pairs = d // LANES
    rows = tiles * Q_TILE
    assert pad % rows == 0 and s % rows == 0
    tq, tk = Q_TILE, KEY_BLOCK
    kr = lax.broadcasted_iota(jnp.int32, (tk, tk), 0)
    kc = lax.broadcasted_iota(jnp.int32, (tk, tk), 1)
    tri = jnp.where(kr > kc, -1.0, 0.0).astype(BF16)
    assert tq <= LANES and tk % LANES == 0
    mr = lax.broadcasted_iota(jnp.int32, (2 * tq, LANES), 0) & (tq - 1)
    mc = lax.broadcasted_iota(jnp.int32, (2 * tq, LANES), 1)
    bias = jnp.where(mc - (LANES - tq) < mr, 0.0, MASKED).astype(F32)
    const = lambda b, p, i: (0, 0)
    return pl.pallas_call(
        functools.partial(_sb_attn_kernel, tiles=tiles, pad=pad),
        grid=(bsz, pairs, s // rows),
        in_specs=[
            pl.BlockSpec((None, rows, LANES), lambda b, p, i: (b, i + pad // rows, p)),
            pl.BlockSpec((None, s_pad, LANES), lambda b, p, i: (b, 0, pairs + p)),
            pl.BlockSpec((None, s_pad, LANES), lambda b, p, i: (b, 0, 2 * pairs + p)),
            pl.BlockSpec((tk, tk), const),
            pl.BlockSpec((2 * tq, LANES), const),
        ],
        out_specs=pl.BlockSpec((None, rows, LANES), lambda b, p, i: (b, i, p)),
        out_shape=jax.ShapeDtypeStruct((bsz, s, d), BF16),
        scratch_shapes=[
            pltpu.VMEM((tiles, Q_TILE, LANES), F32),
            pltpu.VMEM((tiles, 2 * Q_TILE, 1), F32),
        ],
        compiler_params=pltpu.CompilerParams(
            dimension_semantics=("arbitrary", "arbitrary", "arbitrary"),
            vmem_limit_bytes=VMEM_LIMIT),
        name="sb_attention",
    )(qkv, qkv, qkv, tri, bias)


def kernel(x, norm_mix_g, norm_ffn_g, a_w_in, a_conv_w, a_conv_b, a_w_r, a_b_r, a_w_i, a_b_i,
           a_lambda, a_w_out, b_w_qkv, b_w_out, ffn_w_gate, ffn_w_up, ffn_w_down, final_g):
    bsz, s, d = x.shape
    m = bsz * s
    row = lambda v: v.reshape(1, -1)
    bf = lambda w: w.astype(BF16)

    x = _rglru_ffn_layer(x, row(norm_mix_g[0]), bf(a_w_in[0]), a_conv_w[0], row(a_conv_b[0]),
                         bf(a_w_r[0]), row(a_b_r[0]), bf(a_w_i[0]), row(a_b_i[0]),
                         row(a_lambda[0]), bf(a_w_out[0]), row(norm_ffn_g[0]),
                         ffn_w_gate, ffn_w_up, ffn_w_down, 0, tm=512, tf=256)

    tiles = 32
    pad = tiles * Q_TILE
    qkv = _qkv_proj(x.reshape(bsz, s, d), row(norm_mix_g[1]), b_w_qkv[0], tm=1024, pad=pad)
    att = _sb_attention(qkv, d=d, pad=pad, tiles=tiles)
    out = _ffn(x, row(norm_ffn_g[1]), ffn_w_gate, ffn_w_up, ffn_w_down, 1,
               att=att.reshape(m, d), w_o=b_w_out[0], final_g=row(final_g), tm=1024, tf=256)
    return out.reshape(bsz, s, d)
```

```python
import functools
import math

import jax
import jax.numpy as jnp
from jax import lax
from jax.experimental import pallas as pl
from jax.experimental.pallas import tpu as pltpu

RMS_EPS = 1e-6
RG_C = 8.0
CONV_W = 4
RG_BLOCKS = 4
SB_HEAD_DIM = 64
LANES = 128
SUBLANES = 8
ROW_CHUNK = 256
Q_TILE = 64
KEY_BLOCK = 256
LOG2E = 1.4426950408889634
GELU_C0 = math.sqrt(2.0 / math.pi)
EXIT_LOG = -130.0
MASKED = -1e30
VMEM_LIMIT = 56 * 1024 * 1024

RGLRU_TM = 512
FFN_TM = 1024
FFN_TF = 256
QKV_TM = 1024
ATTN_TILES = 32

BF16 = jnp.bfloat16
F32 = jnp.float32


def _rmsnorm(x, g):
    ms = jnp.mean(x * x, axis=-1, keepdims=True)
    return x * lax.rsqrt(ms + RMS_EPS) * g


def _dot(a, b):
    return jnp.dot(a, b, preferred_element_type=F32)


def _gelu_tanh(x):
    inner = x * (GELU_C0 + (GELU_C0 * 0.044715) * (x * x))
    hx = 0.5 * x
    return hx + hx * jnp.tanh(inner)


def _softplus(z):
    return jnp.maximum(z, 0.0) + jnp.log(1.0 + jnp.exp(-jnp.abs(z)))


def _rglru_kernel(x_ref, g_ref, win_ref, cw_ref, cb_ref, wr_ref, br_ref, wi_ref, bi_ref,
                  lam_ref, wout_ref, o_ref, xpad_ref, a_ref, u_ref, hs_ref, h_ref):
    tm, d = x_ref.shape
    bw = d // RG_BLOCKS
    pad = SUBLANES

    @pl.when(pl.program_id(1) == 0)
    def _():
        xpad_ref[0:pad, :] = jnp.zeros((pad, d), F32)
        h_ref[...] = jnp.zeros_like(h_ref)

    half_rate = (0.5 * RG_C) * -_softplus(-lam_ref[...])

    gates = []
    h = h_ref[...]
    for c0 in range(0, tm, ROW_CHUNK):
        rows = slice(c0, c0 + ROW_CHUNK)
        hb = _rmsnorm(x_ref[rows, :], g_ref[...]).astype(BF16)
        xpad_ref[pad + c0:pad + c0 + ROW_CHUNK, :] = _dot(hb, win_ref[:, d:])

        xc = cb_ref[...] + cw_ref[CONV_W - 1:CONV_W, :] * xpad_ref[pad + c0:pad + c0 + ROW_CHUNK, :]
        for k in range(CONV_W - 1):
            off = pad + c0 - (CONV_W - 1) + k
            xc = xc + cw_ref[k:k + 1, :] * xpad_ref[off:off + ROW_CHUNK, :]

        xcb = xc.astype(BF16)
        for n in range(RG_BLOCKS):
            sl = slice(n * bw, (n + 1) * bw)
            tr = jnp.tanh(_dot(xcb[:, sl], wr_ref[n]) + br_ref[:, sl])
            ti = jnp.tanh(_dot(xcb[:, sl], wi_ref[n]) + bi_ref[:, sl])
            log_a = tr * half_rate[:, sl] + half_rate[:, sl]
            a = jnp.exp(log_a)
            mult = jnp.exp2(jnp.log(jnp.maximum(jnp.tanh(log_a) * (-1.0 - a * a), 0.0))
                            * (0.5 * LOG2E))
            a_ref[rows, sl] = a
            u_ref[rows, sl] = (mult * xc[:, sl]) * (0.5 * ti + 0.5)

        for t in range(c0, c0 + ROW_CHUNK):
            h = a_ref[t:t + 1, :] * h + u_ref[t:t + 1, :]
            hs_ref[t:t + 1, :] = h

        gates.append(_gelu_tanh(_dot(hb, win_ref[:, :d])))
    h_ref[...] = h
    xpad_ref[0:pad, :] = xpad_ref[tm:tm + pad, :]

    for c, c0 in enumerate(range(0, tm, ROW_CHUNK)):
        rows = slice(c0, c0 + ROW_CHUNK)
        y = (hs_ref[rows, :] * gates[c]).astype(BF16)
        o_ref[rows, :] = x_ref[rows, :] + _dot(y, wout_ref[...])


def _rglru_mixer(x, g, w_in, conv_w, conv_b, w_r, b_r, w_i, b_i, lam, w_out, *, tm):
    bsz, s, d = x.shape
    assert s % tm == 0 and tm % ROW_CHUNK == 0
    bw = d // RG_BLOCKS
    const2 = lambda b, i: (0, 0)
    const3 = lambda b, i: (0, 0, 0)
    vec = pl.BlockSpec((1, d), const2)
    return pl.pallas_call(
        _rglru_kernel,
        grid=(bsz, s // tm),
        in_specs=[
            pl.BlockSpec((None, tm, d), lambda b, i: (b, i, 0)),
            vec,
            pl.BlockSpec((d, 2 * d), const2),
            pl.BlockSpec((CONV_W, d), const2),
            vec,
            pl.BlockSpec((RG_BLOCKS, bw, bw), const3),
            vec,
            pl.BlockSpec((RG_BLOCKS, bw, bw), const3),
            vec,
            vec,
            pl.BlockSpec((d, d), const2),
        ],
        out_specs=pl.BlockSpec((None, tm, d), lambda b, i: (b, i, 0)),
        out_shape=jax.ShapeDtypeStruct((bsz, s, d), F32),
        scratch_shapes=[
            pltpu.VMEM((tm + SUBLANES, d), F32),
            pltpu.VMEM((tm, d), F32),
            pltpu.VMEM((tm, d), F32),
            pltpu.VMEM((tm, d), F32),
            pltpu.VMEM((1, d), F32),
        ],
        compiler_params=pltpu.CompilerParams(
            dimension_semantics=("arbitrary", "arbitrary"),
            vmem_limit_bytes=VMEM_LIMIT),
        name="rglru_mixer",
    )(x, g, w_in, conv_w, conv_b, w_r, b_r, w_i, b_i, lam, w_out)


def _ffn_kernel(*refs, pre_proj, final_norm, nchunk):
    refs = list(refs)
    x_ref = refs.pop(0)
    if pre_proj:
        att_ref = refs.pop(0)
        wo_ref = refs.pop(0)
    g_ref, wg_ref, wu_ref, wd_ref = refs[:4]
    refs = refs[4:]
    if final_norm:
        fg_ref = refs.pop(0)
    o_ref, wg_s, wu_s, wd_s, hb_s = refs

    t = pl.program_id(0)

    def start_tile():
        x = x_ref[...]
        if pre_proj:
            x = x + _dot(att_ref[...], wo_ref[...].astype(BF16))
        o_ref[...] = x
        return _rmsnorm(x, g_ref[...]).astype(BF16)

    def add_chunk(h, c):
        gate = _dot(h, wg_s[c])
        up = _dot(h, wu_s[c])
        act = (gate * jax.nn.sigmoid(gate) * up).astype(BF16)
        o_ref[...] += _dot(act, wd_s[c])

    def finish_tile():
        if final_norm:
            o_ref[...] = _rmsnorm(o_ref[...], fg_ref[...])

    @pl.when(t < nchunk)
    def _():
        wg_s[t] = wg_ref[...].astype(BF16)
        wu_s[t] = wu_ref[...].astype(BF16)
        wd_s[t] = wd_ref[...].astype(BF16)

        @pl.when(t == 0)
        def _():
            hb_s[...] = start_tile()

        add_chunk(hb_s[...], t)

        @pl.when(t == nchunk - 1)
        def _():
            finish_tile()

    @pl.when(t >= nchunk)
    def _():
        h = start_tile()
        for c in range(nchunk):
            add_chunk(h, c)
        finish_tile()


def _ffn(x, g, w_gate, w_up, w_down, layer, *, att=None, w_o=None, final_g=None, tm, tf):
    m, d = x.shape
    dff = w_gate.shape[2]
    assert dff % tf == 0 and m % tm == 0
    nchunk = dff // tf
    pre_proj = att is not None
    final_norm = final_g is not None
    resident = dict(pipeline_mode=pl.Buffered(1))
    tile_of = lambda t: jnp.maximum(t - (nchunk - 1), 0)
    chunk_of = lambda t: jnp.minimum(t, nchunk - 1)
    row = pl.BlockSpec((tm, d), lambda t: (tile_of(t), 0))
    vec = pl.BlockSpec((1, d), lambda t: (0, 0), **resident)
    args, specs = [x], [row]
    if pre_proj:
        args += [att, w_o]
        specs += [row, pl.BlockSpec((d, d), lambda t: (0, 0), **resident)]
    args += [g, w_gate, w_up, w_down]
    specs += [vec,
              pl.BlockSpec((None, d, tf), lambda t: (layer, 0, chunk_of(t))),
              pl.BlockSpec((None, d, tf), lambda t: (layer, 0, chunk_of(t))),
              pl.BlockSpec((None, tf, d), lambda t: (layer, chunk_of(t), 0))]
    if final_norm:
        args.append(final_g)
        specs.append(vec)
    return pl.pallas_call(
        functools.partial(_ffn_kernel, pre_proj=pre_proj, final_norm=final_norm, nchunk=nchunk),
        grid=(nchunk + m // tm - 1,),
        in_specs=specs,
        out_specs=row,
        out_shape=jax.ShapeDtypeStruct((m, d), F32),
        scratch_shapes=[
            pltpu.VMEM((nchunk, d, tf), BF16),
            pltpu.VMEM((nchunk, d, tf), BF16),
            pltpu.VMEM((nchunk, tf, d), BF16),
            pltpu.VMEM((tm, d), BF16),
        ],
        compiler_params=pltpu.CompilerParams(
            dimension_semantics=("arbitrary",),
            vmem_limit_bytes=VMEM_LIMIT),
        name="ffn_attn_out" if pre_proj else "ffn",
    )(*args)


def _qkv_kernel(x_ref, g_ref, w_ref, o_ref, w_s, *, q_scale, nload, pad_tiles, tiles_per_seq):
    d = x_ref.shape[1]
    t = pl.program_id(0)

    @pl.when(t < nload)
    def _():
        w_s[t] = w_ref[...].astype(BF16)

    item = jnp.maximum(t - (nload - pad_tiles), 0)
    is_pad = lax.rem(item, tiles_per_seq) < pad_tiles

    @pl.when(is_pad)
    def _():
        o_ref[...] = jnp.zeros_like(o_ref)

    @pl.when(jnp.logical_not(is_pad))
    def _():
        h = _rmsnorm(x_ref[...], g_ref[...]).astype(BF16)
        for c in range(nload):
            y = _dot(h, w_s[c])
            if c == 0:
                y = y * q_scale
            o_ref[:, c * d:(c + 1) * d] = y.astype(o_ref.dtype)


def _qkv_proj(x, g, w_qkv, *, tm, pad):
    bsz, s, d = x.shape
    n = w_qkv.shape[1]
    nload = n // d
    pad_tiles = pad // tm
    assert pad % tm == 0 and s % tm == 0 and 1 <= pad_tiles <= nload
    tiles_per_seq = s // tm + pad_tiles
    resident = dict(pipeline_mode=pl.Buffered(1))
    item = lambda t: jnp.maximum(t - (nload - pad_tiles), 0)
    seq_of = lambda t: item(t) // tiles_per_seq
    tile_of = lambda t: item(t) % tiles_per_seq
    return pl.pallas_call(
        functools.partial(_qkv_kernel, q_scale=LOG2E / math.sqrt(SB_HEAD_DIM), nload=nload,
                          pad_tiles=pad_tiles, tiles_per_seq=tiles_per_seq),
        grid=(nload - pad_tiles + bsz * tiles_per_seq,),
        in_specs=[
            pl.BlockSpec((None, tm, d),
                         lambda t: (seq_of(t), jnp.maximum(tile_of(t) - pad_tiles, 0), 0)),
            pl.BlockSpec((1, d), lambda t: (0, 0), **resident),
            pl.BlockSpec((d, d), lambda t: (0, jnp.minimum(t, nload - 1))),
        ],
        out_specs=pl.BlockSpec((None, tm, n), lambda t: (seq_of(t), tile_of(t), 0)),
        out_shape=jax.ShapeDtypeStruct((bsz, s + pad, n), BF16),
        scratch_shapes=[pltpu.VMEM((nload, d, d), BF16)],
        compiler_params=pltpu.CompilerParams(
            dimension_semantics=("arbitrary",),
            vmem_limit_bytes=VMEM_LIMIT),
        name="qkv_proj",
    )(x, g, w_qkv)


def _sb_attn_kernel(q_ref, k_ref, v_ref, tri_ref, bias_ref, o_ref, acc_ref, c_ref, *, tiles, pad):
    tq, tk = Q_TILE, KEY_BLOCK
    step = pl.program_id(2)

    lane = lax.broadcasted_iota(jnp.int32, (tq, LANES), 1)
    head0, head1 = lane < SB_HEAD_DIM, lane >= SB_HEAD_DIM

    def scores(q, r0, first):
        zq = jnp.zeros_like(q)
        qs = jnp.concatenate([jnp.where(head0, q, zq), jnp.where(head1, q, zq)], axis=0)
        k = k_ref[pl.ds(r0, tk), :]
        z = lax.dot_general(qs, k, (((1,), (1,)), ((), ())), preferred_element_type=F32)
        if first:
            z = jnp.concatenate([z[:, :tk - LANES], z[:, tk - LANES:] + bias_ref[...]], axis=1)
        return z

    def weights(z):
        zneg, zpos = jnp.minimum(z, 0.0), jnp.maximum(z, 0.0)
        lg = jnp.log(1.0 + jnp.exp2(zneg - zpos)) * LOG2E
        sp = zpos + lg
        sfx = _dot(sp.astype(BF16), tri_ref[...])
        w = jnp.exp2((zneg - lg) + sfx)
        total = sfx[:, 0:1] - sp[:, 0:1]
        return w.astype(BF16), total

    def weighted_values(w, r0):
        return _dot(w, v_ref[pl.ds(r0, tk), :])

    rows2 = 2 * tq
    ends = [pl.multiple_of((step * tiles + g + 1) * tq + pad, tq) for g in range(tiles)]
    w, total = weights(jnp.concatenate(
        [scores(q_ref[g * tq:(g + 1) * tq, :], ends[g] - tk, True) for g in range(tiles)],
        axis=0))
    for g in range(tiles):
        pv = weighted_values(w[g * rows2:(g + 1) * rows2], ends[g] - tk)
        acc = jnp.where(head0, pv[:tq], pv[tq:])
        o_ref[g * tq:(g + 1) * tq, :] = acc.astype(o_ref.dtype)
        acc_ref[g] = acc
        c_ref[g] = total[g * rows2:(g + 1) * rows2]

    @pl.when(jnp.max(total) > EXIT_LOG)
    def _():
        def more(carry):
            end, cm = carry
            return (end > pad) & (cm > EXIT_LOG)

        def walk(g, _):
            q0 = pl.multiple_of(g * tq, tq)
            q = q_ref[pl.ds(q0, tq), :]

            def body(carry):
                end, _ = carry
                r0 = pl.multiple_of(end - tk, tq)
                w, total = weights(scores(q, r0, False))
                c = c_ref[g]
                pv = jnp.exp2(c) * weighted_values(w, r0)
                acc_ref[g] += jnp.where(head0, pv[:tq], pv[tq:])
                c = c + total
                c_ref[g] = c
                return end - tk, jnp.max(c)

            first_start = (step * tiles + g + 1) * tq + pad - tk
            lax.while_loop(more, body, (first_start, jnp.max(c_ref[g])))
            o_ref[pl.ds(q0, tq), :] = acc_ref[g].astype(o_ref.dtype)
            return 0

        lax.fori_loop(0, tiles, walk, 0)


def _sb_attention(qkv, *, d, pad, tiles):
    bsz, s_pad, _ = qkv.shape
    s = s_pad - pad
    pairs = d // LANES
    rows = tiles * Q_TILE
    assert pad % rows == 0 and s % rows == 0
    tq, tk = Q_TILE, KEY_BLOCK
    kr = lax.broadcasted_iota(jnp.int32, (tk, tk), 0)
    kc = lax.broadcasted_iota(jnp.int32, (tk, tk), 1)
    tri = jnp.where(kr > kc, -1.0, 0.0).astype(BF16)
    assert tq <= LANES and tk % LANES == 0
    mr = lax.broadcasted_iota(jnp.int32, (2 * tq, LANES), 0) & (tq - 1)
    mc = lax.broadcasted_iota(jnp.int32, (2 * tq, LANES), 1)
    bias = jnp.where(mc - (LANES - tq) < mr, 0.0, MASKED).astype(F32)
    const = lambda b, p, i: (0, 0)
    return pl.pallas_call(
        functools.partial(_sb_attn_kernel, tiles=tiles, pad=pad),
        grid=(bsz, pairs, s // rows),
        in_specs=[
            pl.BlockSpec((None, rows, LANES), lambda b, p, i: (b, i + pad // rows, p)),
            pl.BlockSpec((None, s_pad, LANES), lambda b, p, i: (b, 0, pairs + p)),
            pl.BlockSpec((None, s_pad, LANES), lambda b, p, i: (b, 0, 2 * pairs + p)),
            pl.BlockSpec((tk, tk), const),
            pl.BlockSpec((2 * tq, LANES), const),
        ],
        out_specs=pl.BlockSpec((None, rows, LANES), lambda b, p, i: (b, i, p)),
        out_shape=jax.ShapeDtypeStruct((bsz, s, d), BF16),
        scratch_shapes=[
            pltpu.VMEM((tiles, Q_TILE, LANES), F32),
            pltpu.VMEM((tiles, 2 * Q_TILE, 1), F32),
        ],
        compiler_params=pltpu.CompilerParams(
            dimension_semantics=("arbitrary", "arbitrary", "arbitrary"),
            vmem_limit_bytes=VMEM_LIMIT),
        name="sb_attention",
    )(qkv, qkv, qkv, tri, bias)


def kernel(x, norm_mix_g, norm_ffn_g, a_w_in, a_conv_w, a_conv_b, a_w_r, a_b_r, a_w_i, a_b_i,
           a_lambda, a_w_out, b_w_qkv, b_w_out, ffn_w_gate, ffn_w_up, ffn_w_down, final_g):
    bsz, s, d = x.shape
    m = bsz * s
    row = lambda v: v.reshape(1, -1)
    bf = lambda w: w.astype(BF16)

    x = _rglru_mixer(x, row(norm_mix_g[0]), bf(a_w_in[0]), a_conv_w[0], row(a_conv_b[0]),
                     bf(0.5 * a_w_r[0]), row(0.5 * a_b_r[0]), bf(0.5 * a_w_i[0]),
                     row(0.5 * a_b_i[0]), row(a_lambda[0]), bf(a_w_out[0]), tm=RGLRU_TM)
    x = _ffn(x.reshape(m, d), row(norm_ffn_g[0]), ffn_w_gate, ffn_w_up, ffn_w_down, 0,
             tm=FFN_TM, tf=FFN_TF)

    pad = ATTN_TILES * Q_TILE
    qkv = _qkv_proj(x.reshape(bsz, s, d), row(norm_mix_g[1]), b_w_qkv[0], tm=QKV_TM, pad=pad)
    att = _sb_attention(qkv, d=d, pad=pad, tiles=ATTN_TILES)
    out = _ffn(x, row(norm_ffn_g[1]), ffn_w_gate, ffn_w_up, ffn_w_down, 1,
               att=att.reshape(m, d), w_o=b_w_out[0], final_g=row(final_g), tm=FFN_TM, tf=FFN_TF)
    return out.reshape(bsz, s, d)
```

```python
import functools
import math

import jax
import jax.numpy as jnp
from jax import lax
from jax.experimental import pallas as pl
from jax.experimental.pallas import tpu as pltpu

RMS_EPS = 1e-6
RG_C = 8.0
CONV_W = 4
RG_BLOCKS = 4
SB_HEAD_DIM = 64
LANES = 128
SUBLANES = 8
ROW_CHUNK = 256
Q_TILE = 64
KEY_BLOCK = 256
LOG2E = 1.4426950408889634
GELU_C0 = math.sqrt(2.0 / math.pi)
EXIT_LOG = -130.0
MASKED = -1e30
VMEM_LIMIT = 56 * 1024 * 1024

RGLRU_TM = 512
FFN_TM = 1024
FFN_TF = 256
QKV_TM = 1024
ATTN_TILES = 32

BF16 = jnp.bfloat16
F32 = jnp.float32


def _rmsnorm(x, g):
    ms = jnp.mean(x * x, axis=-1, keepdims=True)
    return x * lax.rsqrt(ms + RMS_EPS) * g


def _dot(a, b):
    return jnp.dot(a, b, preferred_element_type=F32)


def _gelu_tanh(x):
    inner = x * (GELU_C0 + (GELU_C0 * 0.044715) * (x * x))
    hx = 0.5 * x
    return hx + hx * jnp.tanh(inner)


def _softplus(z):
    return jnp.maximum(z, 0.0) + jnp.log(1.0 + jnp.exp(-jnp.abs(z)))


def _rglru_kernel(x_ref, g_ref, win_ref, cw_ref, cb_ref, wr_ref, br_ref, wi_ref, bi_ref,
                  lam_ref, wout_ref, o_ref, xpad_ref, a_ref, u_ref, hs_ref, h_ref):
    tm, d = x_ref.shape
    bw = d // RG_BLOCKS
    pad = SUBLANES

    @pl.when(pl.program_id(1) == 0)
    def _():
        xpad_ref[0:pad, :] = jnp.zeros((pad, d), F32)
        h_ref[...] = jnp.zeros_like(h_ref)

    half_rate = (0.5 * RG_C) * -_softplus(-lam_ref[...])

    gates = []
    h = h_ref[...]
    for c0 in range(0, tm, ROW_CHUNK):
        rows = slice(c0, c0 + ROW_CHUNK)
        hb = _rmsnorm(x_ref[rows, :], g_ref[...]).astype(BF16)
        xpad_ref[pad + c0:pad + c0 + ROW_CHUNK, :] = _dot(hb, win_ref[:, d:])

        xc = cb_ref[...] + cw_ref[CONV_W - 1:CONV_W, :] * xpad_ref[pad + c0:pad + c0 + ROW_CHUNK, :]
        for k in range(CONV_W - 1):
            off = pad + c0 - (CONV_W - 1) + k
            xc = xc + cw_ref[k:k + 1, :] * xpad_ref[off:off + ROW_CHUNK, :]

        xcb = xc.astype(BF16)
        for n in range(RG_BLOCKS):
            sl = slice(n * bw, (n + 1) * bw)
            tr = jnp.tanh(_dot(xcb[:, sl], wr_ref[n]) + br_ref[:, sl])
            ti = jnp.tanh(_dot(xcb[:, sl], wi_ref[n]) + bi_ref[:, sl])
            log_a = tr * half_rate[:, sl] + half_rate[:, sl]
            a = jnp.exp(log_a)
            mult = jnp.exp2(jnp.log(jnp.maximum(jnp.tanh(log_a) * (-1.0 - a * a), 0.0))
                            * (0.5 * LOG2E))
            a_ref[rows, sl] = a
            u_ref[rows, sl] = (mult * xc[:, sl]) * (0.5 * ti + 0.5)

        for t in range(c0, c0 + ROW_CHUNK):
            h = a_ref[t:t + 1, :] * h + u_ref[t:t + 1, :]
            hs_ref[t:t + 1, :] = h

        gates.append(_gelu_tanh(_dot(hb, win_ref[:, :d])))
    h_ref[...] = h
    xpad_ref[0:pad, :] = xpad_ref[tm:tm + pad, :]

    for c, c0 in enumerate(range(0, tm, ROW_CHUNK)):
        rows = slice(c0, c0 + ROW_CHUNK)
        y = (hs_ref[rows, :] * gates[c]).astype(BF16)
        o_ref[rows, :] = x_ref[rows, :] + _dot(y, wout_ref[...])


def _rglru_mixer(x, g, w_in, conv_w, conv_b, w_r, b_r, w_i, b_i, lam, w_out, *, tm):
    bsz, s, d = x.shape
    assert s % tm == 0 and tm % ROW_CHUNK == 0
    bw = d // RG_BLOCKS
    const2 = lambda b, i: (0, 0)
    const3 = lambda b, i: (0, 0, 0)
    vec = pl.BlockSpec((1, d), const2)
    return pl.pallas_call(
        _rglru_kernel,
        grid=(bsz, s // tm),
        in_specs=[
            pl.BlockSpec((None, tm, d), lambda b, i: (b, i, 0)),
            vec,
            pl.BlockSpec((d, 2 * d), const2),
            pl.BlockSpec((CONV_W, d), const2),
            vec,
            pl.BlockSpec((RG_BLOCKS, bw, bw), const3),
            vec,
            pl.BlockSpec((RG_BLOCKS, bw, bw), const3),
            vec,
            vec,
            pl.BlockSpec((d, d), const2),
        ],
        out_specs=pl.BlockSpec((None, tm, d), lambda b, i: (b, i, 0)),
        out_shape=jax.ShapeDtypeStruct((bsz, s, d), F32),
        scratch_shapes=[
            pltpu.VMEM((tm + SUBLANES, d), F32),
            pltpu.VMEM((tm, d), F32),
            pltpu.VMEM((tm, d), F32),
            pltpu.VMEM((tm, d), F32),
            pltpu.VMEM((1, d), F32),
        ],
        compiler_params=pltpu.CompilerParams(
            dimension_semantics=("arbitrary", "arbitrary"),
            vmem_limit_bytes=VMEM_LIMIT),
        name="rglru_mixer",
    )(x, g, w_in, conv_w, conv_b, w_r, b_r, w_i, b_i, lam, w_out)


def _ffn_kernel(*refs, pre_proj, final_norm, nchunk):
    refs = list(refs)
    x_ref = refs.pop(0)
    if pre_proj:
        att_ref = refs.pop(0)
        wo_ref = refs.pop(0)
    g_ref, wg_ref, wu_ref, wd_ref = refs[:4]
    refs = refs[4:]
    if final_norm:
        fg_ref = refs.pop(0)
    o_ref, wg_s, wu_s, wd_s, hb_s = refs

    t = pl.program_id(0)

    def start_tile():
        x = x_ref[...]
        if pre_proj:
            x = x + _dot(att_ref[...], wo_ref[...].astype(BF16))
        o_ref[...] = x
        return _rmsnorm(x, g_ref[...]).astype(BF16)

    def add_chunk(h, c):
        gate = _dot(h, wg_s[c])
        up = _dot(h, wu_s[c])
        act = (gate * jax.nn.sigmoid(gate) * up).astype(BF16)
        o_ref[...] += _dot(act, wd_s[c])

    def finish_tile():
        if final_norm:
            o_ref[...] = _rmsnorm(o_ref[...], fg_ref[...])

    @pl.when(t < nchunk)
    def _():
        wg_s[t] = wg_ref[...].astype(BF16)
        wu_s[t] = wu_ref[...].astype(BF16)
        wd_s[t] = wd_ref[...].astype(BF16)

        @pl.when(t == 0)
        def _():
            hb_s[...] = start_tile()

        add_chunk(hb_s[...], t)

        @pl.when(t == nchunk - 1)
        def _():
            finish_tile()

    @pl.when(t >= nchunk)
    def _():
        h = start_tile()
        for c in range(nchunk):
            add_chunk(h, c)
        finish_tile()


def _ffn(x, g, w_gate, w_up, w_down, layer, *, att=None, w_o=None, final_g=None, tm, tf):
    m, d = x.shape
    dff = w_gate.shape[2]
    assert dff % tf == 0 and m % tm == 0
    nchunk = dff // tf
    pre_proj = att is not None
    final_norm = final_g is not None
    resident = dict(pipeline_mode=pl.Buffered(1))
    tile_of = lambda t: jnp.maximum(t - (nchunk - 1), 0)
    chunk_of = lambda t: jnp.minimum(t, nchunk - 1)
    row = pl.BlockSpec((tm, d), lambda t: (tile_of(t), 0))
    vec = pl.BlockSpec((1, d), lambda t: (0, 0), **resident)
    args, specs = [x], [row]
    if pre_proj:
        args += [att, w_o]
        specs += [row, pl.BlockSpec((d, d), lambda t: (0, 0), **resident)]
    args += [g, w_gate, w_up, w_down]
    specs += [vec,
              pl.BlockSpec((None, d, tf), lambda t: (layer, 0, chunk_of(t))),
              pl.BlockSpec((None, d, tf), lambda t: (layer, 0, chunk_of(t))),
              pl.BlockSpec((None, tf, d), lambda t: (layer, chunk_of(t), 0))]
    if final_norm:
        args.append(final_g)
        specs.append(vec)
    return pl.pallas_call(
        functools.partial(_ffn_kernel, pre_proj=pre_proj, final_norm=final_norm, nchunk=nchunk),
        grid=(nchunk + m // tm - 1,),
        in_specs=specs,
        out_specs=row,
        out_shape=jax.ShapeDtypeStruct((m, d), F32),
        scratch_shapes=[
            pltpu.VMEM((nchunk, d, tf), BF16),
            pltpu.VMEM((nchunk, d, tf), BF16),
            pltpu.VMEM((nchunk, tf, d), BF16),
            pltpu.VMEM((tm, d), BF16),
        ],
        compiler_params=pltpu.CompilerParams(
            dimension_semantics=("arbitrary",),
            vmem_limit_bytes=VMEM_LIMIT),
        name="ffn_attn_out" if pre_proj else "ffn",
    )(*args)


def _qkv_kernel(x_ref, g_ref, w_ref, o_ref, w_s, *, q_scale, nload, pad_tiles, tiles_per_seq):
    d = x_ref.shape[1]
    t = pl.program_id(0)

    @pl.when(t < nload)
    def _():
        w_s[t] = w_ref[...].astype(BF16)

    item = jnp.maximum(t - (nload - pad_tiles), 0)
    is_pad = lax.rem(item, tiles_per_seq) < pad_tiles

    @pl.when(is_pad)
    def _():
        o_ref[...] = jnp.zeros_like(o_ref)

    @pl.when(jnp.logical_not(is_pad))
    def _():
        h = _rmsnorm(x_ref[...], g_ref[...]).astype(BF16)
        for c in range(nload):
            y = _dot(h, w_s[c])
            if c == 0:
                y = y * q_scale
            o_ref[:, c * d:(c + 1) * d] = y.astype(o_ref.dtype)


def _qkv_proj(x, g, w_qkv, *, tm, pad):
    bsz, s, d = x.shape
    n = w_qkv.shape[1]
    nload = n // d
    pad_tiles = pad // tm
    assert pad % tm == 0 and s % tm == 0 and 1 <= pad_tiles <= nload
    tiles_per_seq = s // tm + pad_tiles
    resident = dict(pipeline_mode=pl.Buffered(1))
    item = lambda t: jnp.maximum(t - (nload - pad_tiles), 0)
    seq_of = lambda t: item(t) // tiles_per_seq
    tile_of = lambda t: item(t) % tiles_per_seq
    return pl.pallas_call(
        functools.partial(_qkv_kernel, q_scale=LOG2E / math.sqrt(SB_HEAD_DIM), nload=nload,
                          pad_tiles=pad_tiles, tiles_per_seq=tiles_per_seq),
        grid=(nload - pad_tiles + bsz * tiles_per_seq,),
        in_specs=[
            pl.BlockSpec((None, tm, d),
                         lambda t: (seq_of(t), jnp.maximum(tile_of(t) - pad_tiles, 0), 0)),
            pl.BlockSpec((1, d), lambda t: (0, 0), **resident),
            pl.BlockSpec((d, d), lambda t: (0, jnp.minimum(t, nload - 1))),
        ],
        out_specs=pl.BlockSpec((None, tm, n), lambda t: (seq_of(t), tile_of(t), 0)),
        out_shape=jax.ShapeDtypeStruct((bsz, s + pad, n), BF16),
        scratch_shapes=[pltpu.VMEM((nload, d, d), BF16)],
        compiler_params=pltpu.CompilerParams(
            dimension_semantics=("arbitrary",),
            vmem_limit_bytes=VMEM_LIMIT),
        name="qkv_proj",
    )(x, g, w_qkv)


def _sb_attn_kernel(q_ref, k_ref, v_ref, tri_ref, bias_ref, o_ref, acc_ref, c_ref, *, tiles, pad):
    tq, tk = Q_TILE, KEY_BLOCK
    step = pl.program_id(2)

    lane = lax.broadcasted_iota(jnp.int32, (tq, LANES), 1)
    head0, head1 = lane < SB_HEAD_DIM, lane >= SB_HEAD_DIM

    def scores(q, r0, first):
        zq = jnp.zeros_like(q)
        qs = jnp.concatenate([jnp.where(head0, q, zq), jnp.where(head1, q, zq)], axis=0)
        k = k_ref[pl.ds(r0, tk), :]
        z = lax.dot_general(qs, k, (((1,), (1,)), ((), ())), preferred_element_type=F32)
        if first:
            z = jnp.concatenate([z[:, :tk - LANES], z[:, tk - LANES:] + bias_ref[...]], axis=1)
        return z

    def weights(z):
        lg = jnp.log(1.0 + jnp.exp2(jnp.minimum(z, -z))) * LOG2E
        sp = jnp.maximum(z, 0.0) + lg
        sfx = _dot(sp.astype(BF16), tri_ref[...])
        w = jnp.exp2((z - sp) + sfx)
        total = sfx[:, 0:1] - sp[:, 0:1]
        return w.astype(BF16), total

    def weighted_values(w, r0):
        return _dot(w, v_ref[pl.ds(r0, tk), :])

    rows2 = 2 * tq
    ends = [pl.multiple_of((step * tiles + g + 1) * tq + pad, tq) for g in range(tiles)]
    w, total = weights(jnp.concatenate(
        [scores(q_ref[g * tq:(g + 1) * tq, :], ends[g] - tk, True) for g in range(tiles)],
        axis=0))
    for g in range(tiles):
        pv = weighted_values(w[g * rows2:(g + 1) * rows2], ends[g] - tk)
        acc = jnp.where(head0, pv[:tq], pv[tq:])
        o_ref[g * tq:(g + 1) * tq, :] = acc.astype(o_ref.dtype)
        acc_ref[g] = acc
        c_ref[g] = total[g * rows2:(g + 1) * rows2]

    @pl.when(jnp.max(total) > EXIT_LOG)
    def _():
        def more(carry):
            end, cm = carry
            return (end > pad) & (cm > EXIT_LOG)

        def walk(g, _):
            q0 = pl.multiple_of(g * tq, tq)
            q = q_ref[pl.ds(q0, tq), :]

            def body(carry):
                end, _ = carry
                r0 = pl.multiple_of(end - tk, tq)
                w, total = weights(scores(q, r0, False))
                c = c_ref[g]
                pv = jnp.exp2(c) * weighted_values(w, r0)
                acc_ref[g] += jnp.where(head0, pv[:tq], pv[tq:])
                c = c + total
                c_ref[g] = c
                return end - tk, jnp.max(c)

            first_start = (step * tiles + g + 1) * tq + pad - tk
            lax.while_loop(more, body, (first_start, jnp.max(c_ref[g])))
            o_ref[pl.ds(q0, tq), :] = acc_ref[g].astype(o_ref.dtype)
            return 0

        lax.fori_loop(0, tiles, walk, 0)


def _sb_attention(qkv, *, d, pad, tiles):
    bsz, s_pad, _ = qkv.shape
    s = s_pad - pad
    pairs = d // LANES
    rows = tiles * Q_TILE
    assert pad % rows == 0 and s % rows == 0
    tq, tk = Q_TILE, KEY_BLOCK
    kr = lax.broadcasted_iota(jnp.int32, (tk, tk), 0)
    kc = lax.broadcasted_iota(jnp.int32, (tk, tk), 1)
    tri = jnp.where(kr > kc, -1.0, 0.0).astype(BF16)
    assert tq <= LANES and tk % LANES == 0
    mr = lax.broadcasted_iota(jnp.int32, (2 * tq, LANES), 0) & (tq - 1)
    mc = lax.broadcasted_iota(jnp.int32, (2 * tq, LANES), 1)
    bias = jnp.where(mc - (LANES - tq) < mr, 0.0, MASKED).astype(F32)
    const = lambda b, p, i: (0, 0)
    return pl.pallas_call(
        functools.partial(_sb_attn_kernel, tiles=tiles, pad=pad),
        grid=(bsz, pairs, s // rows),
        in_specs=[
            pl.BlockSpec((None, rows, LANES), lambda b, p, i: (b, i + pad // rows, p)),
            pl.BlockSpec((None, s_pad, LANES), lambda b, p, i: (b, 0, pairs + p)),
            pl.BlockSpec((None, s_pad, LANES), lambda b, p, i: (b, 0, 2 * pairs + p)),
            pl.BlockSpec((tk, tk), const),
            pl.BlockSpec((2 * tq, LANES), const),
        ],
        out_specs=pl.BlockSpec((None, rows, LANES), lambda b, p, i: (b, i, p)),
        out_shape=jax.ShapeDtypeStruct((bsz, s, d), BF16),
        scratch_shapes=[
            pltpu.VMEM((tiles, Q_TILE, LANES), F32),
            pltpu.VMEM((tiles, 2 * Q_TILE, 1), F32),
        ],
        compiler_params=pltpu.CompilerParams(
            dimension_semantics=("arbitrary", "arbitrary", "arbitrary"),
            vmem_limit_bytes=VMEM_LIMIT),
        name="sb_attention",
    )(qkv, qkv, qkv, tri, bias)


def kernel(x, norm_mix_g, norm_ffn_g, a_w_in, a_conv_w, a_conv_b, a_w_r, a_b_r, a_w_i, a_b_i,
           a_lambda, a_w_out, b_w_qkv, b_w_out, ffn_w_gate, ffn_w_up, ffn_w_down, final_g):
    bsz, s, d = x.shape
    m = bsz * s
    row = lambda v: v.reshape(1, -1)
    bf = lambda w: w.astype(BF16)

    x = _rglru_mixer(x, row(norm_mix_g[0]), bf(a_w_in[0]), a_conv_w[0], row(a_conv_b[0]),
                     bf(0.5 * a_w_r[0]), row(0.5 * a_b_r[0]), bf(0.5 * a_w_i[0]),
                     row(0.5 * a_b_i[0]), row(a_lambda[0]), bf(a_w_out[0]), tm=RGLRU_TM)
    x = _ffn(x.reshape(m, d), row(norm_ffn_g[0]), ffn_w_gate, ffn_w_up, ffn_w_down, 0,
             tm=FFN_TM, tf=FFN_TF)

    pad = ATTN_TILES * Q_TILE
    qkv = _qkv_proj(x.reshape(bsz, s, d), row(norm_mix_g[1]), b_w_qkv[0], tm=QKV_TM, pad=pad)
    att = _sb_attention(qkv, d=d, pad=pad, tiles=ATTN_TILES)
    out = _ffn(x, row(norm_ffn_g[1]), ffn_w_gate, ffn_w_up, ffn_w_down, 1,
               att=att.reshape(m, d), w_o=b_w_out[0], final_g=row(final_g), tm=FFN_TM, tf=FFN_TF)
    return out.reshape(bsz, s, d)
```

```python
import functools
import math

import jax
import jax.numpy as jnp
from jax import lax
from jax.experimental import pallas as pl
from jax.experimental.pallas import tpu as pltpu

RMS_EPS = 1e-6
RG_C = 8.0
CONV_W = 4
RG_BLOCKS = 4
SB_HEAD_DIM = 64
LANES = 128
SUBLANES = 8
ROW_CHUNK = 256
Q_TILE = 64
KEY_BLOCK = 256
LOG2E = 1.4426950408889634
GELU_C0 = math.sqrt(2.0 / math.pi)
EXIT_LOG = -130.0
MASKED = -1e30
VMEM_LIMIT = 56 * 1024 * 1024

RGLRU_TM = 512
FFN_TM = 1024
FFN_TF = 256
QKV_TM = 1024
ATTN_TILES = 32

BF16 = jnp.bfloat16
F32 = jnp.float32


def _rmsnorm(x, g):
    ms = jnp.mean(x * x, axis=-1, keepdims=True)
    return x * lax.rsqrt(ms + RMS_EPS) * g


def _dot(a, b):
    return jnp.dot(a, b, preferred_element_type=F32)


def _gelu_tanh(x):
    inner = x * (GELU_C0 + (GELU_C0 * 0.044715) * (x * x))
    hx = 0.5 * x
    return hx + hx * jnp.tanh(inner)


def _softplus(z):
    return jnp.maximum(z, 0.0) + jnp.log(1.0 + jnp.exp(-jnp.abs(z)))


def _rglru_kernel(x_ref, g_ref, win_ref, cw_ref, cb_ref, wr_ref, br_ref, wi_ref, bi_ref,
                  lam_ref, wout_ref, o_ref, xpad_ref, a_ref, u_ref, hs_ref, h_ref):
    tm, d = x_ref.shape
    bw = d // RG_BLOCKS
    pad = SUBLANES

    @pl.when(pl.program_id(1) == 0)
    def _():
        xpad_ref[0:pad, :] = jnp.zeros((pad, d), F32)
        h_ref[...] = jnp.zeros_like(h_ref)

    half_rate = (0.5 * RG_C) * -_softplus(-lam_ref[...])

    gates = []
    h = h_ref[...]
    for c0 in range(0, tm, ROW_CHUNK):
        rows = slice(c0, c0 + ROW_CHUNK)
        hb = _rmsnorm(x_ref[rows, :], g_ref[...]).astype(BF16)
        xpad_ref[pad + c0:pad + c0 + ROW_CHUNK, :] = _dot(hb, win_ref[:, d:])

        xc = cb_ref[...] + cw_ref[CONV_W - 1:CONV_W, :] * xpad_ref[pad + c0:pad + c0 + ROW_CHUNK, :]
        for k in range(CONV_W - 1):
            off = pad + c0 - (CONV_W - 1) + k
            xc = xc + cw_ref[k:k + 1, :] * xpad_ref[off:off + ROW_CHUNK, :]

        xcb = xc.astype(BF16)
        for n in range(RG_BLOCKS):
            sl = slice(n * bw, (n + 1) * bw)
            tr = jnp.tanh(_dot(xcb[:, sl], wr_ref[n]) + br_ref[:, sl])
            ti = jnp.tanh(_dot(xcb[:, sl], wi_ref[n]) + bi_ref[:, sl])
            log_a = tr * half_rate[:, sl] + half_rate[:, sl]
            a = jnp.exp(log_a)
            mult = jnp.exp2(jnp.log(jnp.maximum(jnp.tanh(log_a) * (-1.0 - a * a), 0.0))
                            * (0.5 * LOG2E))
            a_ref[rows, sl] = a
            u_ref[rows, sl] = (mult * xc[:, sl]) * (0.5 * ti + 0.5)

        for t in range(c0, c0 + ROW_CHUNK):
            h = a_ref[t:t + 1, :] * h + u_ref[t:t + 1, :]
            hs_ref[t:t + 1, :] = h

        gates.append(_gelu_tanh(_dot(hb, win_ref[:, :d])))
    h_ref[...] = h
    xpad_ref[0:pad, :] = xpad_ref[tm:tm + pad, :]

    for c, c0 in enumerate(range(0, tm, ROW_CHUNK)):
        rows = slice(c0, c0 + ROW_CHUNK)
        y = (hs_ref[rows, :] * gates[c]).astype(BF16)
        o_ref[rows, :] = x_ref[rows, :] + _dot(y, wout_ref[...])


def _rglru_mixer(x, g, w_in, conv_w, conv_b, w_r, b_r, w_i, b_i, lam, w_out, *, tm):
    bsz, s, d = x.shape
    assert s % tm == 0 and tm % ROW_CHUNK == 0
    bw = d // RG_BLOCKS
    const2 = lambda b, i: (0, 0)
    const3 = lambda b, i: (0, 0, 0)
    vec = pl.BlockSpec((1, d), const2)
    return pl.pallas_call(
        _rglru_kernel,
        grid=(bsz, s // tm),
        in_specs=[
            pl.BlockSpec((None, tm, d), lambda b, i: (b, i, 0)),
            vec,
            pl.BlockSpec((d, 2 * d), const2),
            pl.BlockSpec((CONV_W, d), const2),
            vec,
            pl.BlockSpec((RG_BLOCKS, bw, bw), const3),
            vec,
            pl.BlockSpec((RG_BLOCKS, bw, bw), const3),
            vec,
            vec,
            pl.BlockSpec((d, d), const2),
        ],
        out_specs=pl.BlockSpec((None, tm, d), lambda b, i: (b, i, 0)),
        out_shape=jax.ShapeDtypeStruct((bsz, s, d), F32),
        scratch_shapes=[
            pltpu.VMEM((tm + SUBLANES, d), F32),
            pltpu.VMEM((tm, d), F32),
            pltpu.VMEM((tm, d), F32),
            pltpu.VMEM((tm, d), F32),
            pltpu.VMEM((1, d), F32),
        ],
        compiler_params=pltpu.CompilerParams(
            dimension_semantics=("arbitrary", "arbitrary"),
            vmem_limit_bytes=VMEM_LIMIT),
        name="rglru_mixer",
    )(x, g, w_in, conv_w, conv_b, w_r, b_r, w_i, b_i, lam, w_out)


def _ffn_kernel(*refs, pre_proj, final_norm, nchunk):
    refs = list(refs)
    x_ref = refs.pop(0)
    if pre_proj:
        att_ref = refs.pop(0)
        wo_ref = refs.pop(0)
    g_ref, wg_ref, wu_ref, wd_ref = refs[:4]
    refs = refs[4:]
    if final_norm:
        fg_ref = refs.pop(0)
    o_ref, wg_s, wu_s, wd_s, hb_s = refs

    t = pl.program_id(0)

    def start_tile():
        x = x_ref[...]
        if pre_proj:
            x = x + _dot(att_ref[...], wo_ref[...].astype(BF16))
        o_ref[...] = x
        return _rmsnorm(x, g_ref[...]).astype(BF16)

    def add_chunk(h, c):
        gate = _dot(h, wg_s[c])
        up = _dot(h, wu_s[c])
        act = (gate * jax.nn.sigmoid(gate) * up).astype(BF16)
        o_ref[...] += _dot(act, wd_s[c])

    def finish_tile():
        if final_norm:
            o_ref[...] = _rmsnorm(o_ref[...], fg_ref[...])

    @pl.when(t < nchunk)
    def _():
        wg_s[t] = wg_ref[...].astype(BF16)
        wu_s[t] = wu_ref[...].astype(BF16)
        wd_s[t] = wd_ref[...].astype(BF16)

        @pl.when(t == 0)
        def _():
            hb_s[...] = start_tile()

        add_chunk(hb_s[...], t)

        @pl.when(t == nchunk - 1)
        def _():
            finish_tile()

    @pl.when(t >= nchunk)
    def _():
        h = start_tile()
        for c in range(nchunk):
            add_chunk(h, c)
        finish_tile()


def _ffn(x, g, w_gate, w_up, w_down, layer, *, att=None, w_o=None, final_g=None, tm, tf):
    m, d = x.shape
    dff = w_gate.shape[2]
    assert dff % tf == 0 and m % tm == 0
    nchunk = dff // tf
    pre_proj = att is not None
    final_norm = final_g is not None
    resident = dict(pipeline_mode=pl.Buffered(1))
    tile_of = lambda t: jnp.maximum(t - (nchunk - 1), 0)
    chunk_of = lambda t: jnp.minimum(t, nchunk - 1)
    row = pl.BlockSpec((tm, d), lambda t: (tile_of(t), 0))
    vec = pl.BlockSpec((1, d), lambda t: (0, 0), **resident)
    args, specs = [x], [row]
    if pre_proj:
        args += [att, w_o]
        specs += [row, pl.BlockSpec((d, d), lambda t: (0, 0), **resident)]
    args += [g, w_gate, w_up, w_down]
    specs += [vec,
              pl.BlockSpec((None, d, tf), lambda t: (layer, 0, chunk_of(t))),
              pl.BlockSpec((None, d, tf), lambda t: (layer, 0, chunk_of(t))),
              pl.BlockSpec((None, tf, d), lambda t: (layer, chunk_of(t), 0))]
    if final_norm:
        args.append(final_g)
        specs.append(vec)
    return pl.pallas_call(
        functools.partial(_ffn_kernel, pre_proj=pre_proj, final_norm=final_norm, nchunk=nchunk),
        grid=(nchunk + m // tm - 1,),
        in_specs=specs,
        out_specs=row,
        out_shape=jax.ShapeDtypeStruct((m, d), F32),
        scratch_shapes=[
            pltpu.VMEM((nchunk, d, tf), BF16),
            pltpu.VMEM((nchunk, d, tf), BF16),
            pltpu.VMEM((nchunk, tf, d), BF16),
            pltpu.VMEM((tm, d), BF16),
        ],
        compiler_params=pltpu.CompilerParams(
            dimension_semantics=("arbitrary",),
            vmem_limit_bytes=VMEM_LIMIT),
        name="ffn_attn_out" if pre_proj else "ffn",
    )(*args)


def _qkv_kernel(x_ref, g_ref, w_ref, o_ref, w_s, *, q_scale, nload, pad_tiles, tiles_per_seq):
    d = x_ref.shape[1]
    t = pl.program_id(0)

    @pl.when(t < nload)
    def _():
        w_s[t] = w_ref[...].astype(BF16)

    item = jnp.maximum(t - (nload - pad_tiles), 0)
    is_pad = lax.rem(item, tiles_per_seq) < pad_tiles

    @pl.when(is_pad)
    def _():
        o_ref[...] = jnp.zeros_like(o_ref)

    @pl.when(jnp.logical_not(is_pad))
    def _():
        half = x_ref.shape[0] // 2
        for r0 in (0, half):
            h = _rmsnorm(x_ref[r0:r0 + half, :], g_ref[...]).astype(BF16)
            for c in range(nload):
                y = _dot(h, w_s[c])
                if c == 0:
                    y = y * q_scale
                o_ref[r0:r0 + half, c * d:(c + 1) * d] = y.astype(o_ref.dtype)


def _qkv_proj(x, g, w_qkv, *, tm, pad):
    bsz, s, d = x.shape
    n = w_qkv.shape[1]
    nload = n // d
    pad_tiles = pad // tm
    assert pad % tm == 0 and s % tm == 0 and 1 <= pad_tiles <= nload
    tiles_per_seq = s // tm + pad_tiles
    resident = dict(pipeline_mode=pl.Buffered(1))
    item = lambda t: jnp.maximum(t - (nload - pad_tiles), 0)
    seq_of = lambda t: item(t) // tiles_per_seq
    tile_of = lambda t: item(t) % tiles_per_seq
    return pl.pallas_call(
        functools.partial(_qkv_kernel, q_scale=LOG2E / math.sqrt(SB_HEAD_DIM), nload=nload,
                          pad_tiles=pad_tiles, tiles_per_seq=tiles_per_seq),
        grid=(nload - pad_tiles + bsz * tiles_per_seq,),
        in_specs=[
            pl.BlockSpec((None, tm, d),
                         lambda t: (seq_of(t), jnp.maximum(tile_of(t) - pad_tiles, 0), 0)),
            pl.BlockSpec((1, d), lambda t: (0, 0), **resident),
            pl.BlockSpec((d, d), lambda t: (0, jnp.minimum(t, nload - 1))),
        ],
        out_specs=pl.BlockSpec((None, tm, n), lambda t: (seq_of(t), tile_of(t), 0)),
        out_shape=jax.ShapeDtypeStruct((bsz, s + pad, n), BF16),
        scratch_shapes=[pltpu.VMEM((nload, d, d), BF16)],
        compiler_params=pltpu.CompilerParams(
            dimension_semantics=("arbitrary",),
            vmem_limit_bytes=VMEM_LIMIT),
        name="qkv_proj",
    )(x, g, w_qkv)


def _sb_attn_kernel(q_ref, k_ref, v_ref, tri_ref, bias_ref, o_ref, acc_ref, c_ref, *, tiles, pad):
    tq, tk = Q_TILE, KEY_BLOCK
    step = pl.program_id(2)

    lane = lax.broadcasted_iota(jnp.int32, (tq, LANES), 1)
    head0, head1 = lane < SB_HEAD_DIM, lane >= SB_HEAD_DIM

    def scores(q, r0, first):
        zq = jnp.zeros_like(q)
        qs = jnp.concatenate([jnp.where(head0, q, zq), jnp.where(head1, q, zq)], axis=0)
        k = k_ref[pl.ds(r0, tk), :]
        z = lax.dot_general(qs, k, (((1,), (1,)), ((), ())), preferred_element_type=F32)
        if first:
            z = jnp.concatenate([z[:, :tk - LANES], z[:, tk - LANES:] + bias_ref[...]], axis=1)
        return z

    def weights(z):
        lg = jnp.log(1.0 + jnp.exp2(jnp.minimum(z, -z))) * LOG2E
        sp = jnp.maximum(z, 0.0) + lg
        sfx = _dot(sp.astype(BF16), tri_ref[...])
        w = jnp.exp2((z - sp) + sfx)
        return w.astype(BF16), sfx, sp

    def weighted_values(w, r0, row_scale=None):
        pv = _dot(w, v_ref[pl.ds(r0, tk), :])
        if row_scale is not None:
            pv = row_scale * pv
        return jnp.where(head0, pv[:tq], pv[tq:])

    rows2 = 2 * tq
    ends = [pl.multiple_of((step * tiles + g + 1) * tq + pad, tq) for g in range(tiles)]
    w, sfx, _ = weights(jnp.concatenate(
        [scores(q_ref[g * tq:(g + 1) * tq, :], ends[g] - tk, True) for g in range(tiles)],
        axis=0))
    for g in range(tiles):
        acc = weighted_values(w[g * rows2:(g + 1) * rows2], ends[g] - tk)
        o_ref[g * tq:(g + 1) * tq, :] = acc.astype(o_ref.dtype)

    bound = jnp.max(sfx[:, :LANES], axis=0, keepdims=True)[0:1, 0:1]

    @pl.when(bound[0, 0] > EXIT_LOG)
    def _():
        def more(carry):
            end, cm = carry
            return (end > pad) & (cm > EXIT_LOG)

        def walk(g, _):
            q0 = pl.multiple_of(g * tq, tq)
            q = q_ref[pl.ds(q0, tq), :]
            first_start = (step * tiles + g + 1) * tq + pad - tk
            w, sfx, sp = weights(scores(q, pl.multiple_of(first_start, tq), True))
            acc_ref[...] = weighted_values(w, pl.multiple_of(first_start, tq))
            c = sfx[:, 0:1] - sp[:, 0:1]
            c_ref[...] = c

            def body(carry):
                end, _ = carry
                r0 = pl.multiple_of(end - tk, tq)
                w, sfx, sp = weights(scores(q, r0, False))
                c = c_ref[...]
                acc_ref[...] += weighted_values(w, r0, jnp.exp2(c))
                c = c + (sfx[:, 0:1] - sp[:, 0:1])
                c_ref[...] = c
                return end - tk, jnp.max(c)

            lax.while_loop(more, body, (first_start, jnp.max(c)))
            o_ref[pl.ds(q0, tq), :] = acc_ref[...].astype(o_ref.dtype)
            return 0

        lax.fori_loop(0, tiles, walk, 0)


def _sb_attention(qkv, *, d, pad, tiles):
    bsz, s_pad, _ = qkv.shape
    s = s_pad - pad
    pairs = d // LANES
    rows = tiles * Q_TILE
    assert pad % rows == 0 and s % rows == 0
    tq, tk = Q_TILE, KEY_BLOCK
    kr = lax.broadcasted_iota(jnp.int32, (tk, tk), 0)
    kc = lax.broadcasted_iota(jnp.int32, (tk, tk), 1)
    tri = jnp.where(kr > kc, -1.0, 0.0).astype(BF16)
    assert tq <= LANES and tk % LANES == 0
    mr = lax.broadcasted_iota(jnp.int32, (2 * tq, LANES), 0) & (tq - 1)
    mc = lax.broadcasted_iota(jnp.int32, (2 * tq, LANES), 1)
    bias = jnp.where(mc - (LANES - tq) < mr, 0.0, MASKED).astype(F32)
    const = lambda b, p, i: (0, 0)
    return pl.pallas_call(
        functools.partial(_sb_attn_kernel, tiles=tiles, pad=pad),
        grid=(bsz, pairs, s // rows),
        in_specs=[
            pl.BlockSpec((None, rows, LANES), lambda b, p, i: (b, i + pad // rows, p)),
            pl.BlockSpec((None, s_pad, LANES), lambda b, p, i: (b, 0, pairs + p)),
            pl.BlockSpec((None, s_pad, LANES), lambda b, p, i: (b, 0, 2 * pairs + p)),
            pl.BlockSpec((tk, tk), const),
            pl.BlockSpec((2 * tq, LANES), const),
        ],
        out_specs=pl.BlockSpec((None, rows, LANES), lambda b, p, i: (b, i, p)),
        out_shape=jax.ShapeDtypeStruct((bsz, s, d), BF16),
        scratch_shapes=[
            pltpu.VMEM((Q_TILE, LANES), F32),
            pltpu.VMEM((2 * Q_TILE, 1), F32),
        ],
        compiler_params=pltpu.CompilerParams(
            dimension_semantics=("arbitrary", "arbitrary", "arbitrary"),
            vmem_limit_bytes=VMEM_LIMIT),
        name="sb_attention",
    )(qkv, qkv, qkv, tri, bias)


def kernel(x, norm_mix_g, norm_ffn_g, a_w_in, a_conv_w, a_conv_b, a_w_r, a_b_r, a_w_i, a_b_i,
           a_lambda, a_w_out, b_w_qkv, b_w_out, ffn_w_gate, ffn_w_up, ffn_w_down, final_g):
    bsz, s, d = x.shape
    m = bsz * s
    row = lambda v: v.reshape(1, -1)
    bf = lambda w: w.astype(BF16)

    x = _rglru_mixer(x, row(norm_mix_g[0]), bf(a_w_in[0]), a_conv_w[0], row(a_conv_b[0]),
                     bf(0.5 * a_w_r[0]), row(0.5 * a_b_r[0]), bf(0.5 * a_w_i[0]),
                     row(0.5 * a_b_i[0]), row(a_lambda[0]), bf(a_w_out[0]), tm=RGLRU_TM)
    x = _ffn(x.reshape(m, d), row(norm_ffn_g[0]), ffn_w_gate, ffn_w_up, ffn_w_down, 0,
             tm=FFN_TM, tf=FFN_TF)

    pad = ATTN_TILES * Q_TILE
    qkv = _qkv_proj(x.reshape(bsz, s, d), row(norm_mix_g[1]), b_w_qkv[0], tm=QKV_TM, pad=pad)
    att = _sb_attention(qkv, d=d, pad=pad, tiles=ATTN_TILES)
    out = _ffn(x, row(norm_ffn_g[1]), ffn_w_gate, ffn_w_up, ffn_w_down, 1,
               att=att.reshape(m, d), w_o=b_w_out[0], final_g=row(final_g), tm=FFN_TM, tf=FFN_TF)
    return out.reshape(bsz, s, d)
```

```python
import functools
import math

import jax
import jax.numpy as jnp
from jax import lax
from jax.experimental import pallas as pl
from jax.experimental.pallas import tpu as pltpu

RMS_EPS = 1e-6
RG_C = 8.0
CONV_W = 4
RG_BLOCKS = 4
SB_HEAD_DIM = 64
LANES = 128
SUBLANES = 8
ROW_CHUNK = 256
Q_TILE = 64
KEY_BLOCK = 256
LOG2E = 1.4426950408889634
GELU_C0 = math.sqrt(2.0 / math.pi)
EXIT_LOG = -130.0
MASKED = -1e30
VMEM_LIMIT = 56 * 1024 * 1024

RGLRU_TM = 512
FFN_TM = 1024
FFN_TF = 256
QKV_TM = 1024
ATTN_TILES = 32

BF16 = jnp.bfloat16
F32 = jnp.float32


def _rmsnorm(x, g):
    ms = jnp.mean(x * x, axis=-1, keepdims=True)
    return x * lax.rsqrt(ms + RMS_EPS) * g


def _dot(a, b):
    return jnp.dot(a, b, preferred_element_type=F32)


def _gelu_tanh(x):
    inner = x * (GELU_C0 + (GELU_C0 * 0.044715) * (x * x))
    hx = 0.5 * x
    return hx + hx * jnp.tanh(inner)


def _softplus(z):
    return jnp.maximum(z, 0.0) + jnp.log(1.0 + jnp.exp(-jnp.abs(z)))


def _rglru_step(slot, prev, x_ref, xp_ref, g_ref, win_ref, cw_ref, cb_ref, wr_ref, br_ref, wi_ref,
                bi_ref, lam_ref, wout_ref, o_ref, xpad_ref, a_ref, u_ref, gate_ref, hs_ref, h_ref,
                *, tiles_per_seq, ntiles):
    tm, d = x_ref.shape
    bw = d // RG_BLOCKS
    pad = SUBLANES
    t = pl.program_id(0)

    @pl.when(t == 0)
    def _():
        a_ref[...] = jnp.zeros_like(a_ref)
        u_ref[...] = jnp.zeros_like(u_ref)
        gate_ref[...] = jnp.zeros_like(gate_ref)

    @pl.when(lax.rem(jnp.minimum(t, ntiles - 1), tiles_per_seq) == 0)
    def _():
        xpad_ref[0:pad, :] = jnp.zeros((pad, d), F32)

    @pl.when(lax.rem(jnp.maximum(t - 1, 0), tiles_per_seq) == 0)
    def _():
        h_ref[...] = jnp.zeros_like(h_ref)

    half_rate = (0.5 * RG_C) * -_softplus(-lam_ref[...])
    scan_rows = ROW_CHUNK // RG_BLOCKS

    def recurrence(r0, n, h):
        for r in range(r0, r0 + n):
            h = a_ref[prev, r:r + 1, :] * h + u_ref[prev, r:r + 1, :]
            hs_ref[r:r + 1, :] = h
        return h

    def out_proj(c0):
        rows = slice(c0, c0 + ROW_CHUNK)
        y = (hs_ref[rows, :] * gate_ref[prev, rows, :]).astype(BF16)
        o_ref[rows, :] = xp_ref[rows, :] + _dot(y, wout_ref[...])

    h = h_ref[...]
    for c0 in range(0, tm, ROW_CHUNK):
        rows = slice(c0, c0 + ROW_CHUNK)
        hb = _rmsnorm(x_ref[rows, :], g_ref[...]).astype(BF16)
        xpad_ref[pad + c0:pad + c0 + ROW_CHUNK, :] = _dot(hb, win_ref[:, d:])

        xc = cb_ref[...] + cw_ref[CONV_W - 1:CONV_W, :] * xpad_ref[pad + c0:pad + c0 + ROW_CHUNK, :]
        for k in range(CONV_W - 1):
            off = pad + c0 - (CONV_W - 1) + k
            xc = xc + cw_ref[k:k + 1, :] * xpad_ref[off:off + ROW_CHUNK, :]

        xcb = xc.astype(BF16)
        for n in range(RG_BLOCKS):
            sl = slice(n * bw, (n + 1) * bw)
            tr = jnp.tanh(_dot(xcb[:, sl], wr_ref[n]) + br_ref[:, sl])
            ti = jnp.tanh(_dot(xcb[:, sl], wi_ref[n]) + bi_ref[:, sl])
            log_a = tr * half_rate[:, sl] + half_rate[:, sl]
            a = jnp.exp(log_a)
            mult = jnp.exp2(jnp.log(jnp.maximum(jnp.tanh(log_a) * (-1.0 - a * a), 0.0))
                            * (0.5 * LOG2E))
            a_ref[slot, rows, sl] = a
            u_ref[slot, rows, sl] = (mult * xc[:, sl]) * (0.5 * ti + 0.5)
            h = recurrence(c0 + n * scan_rows, scan_rows, h)

        gate_ref[slot, rows, :] = _gelu_tanh(_dot(hb, win_ref[:, :d]))
        out_proj(c0)
    h_ref[...] = h
    xpad_ref[0:pad, :] = xpad_ref[tm:tm + pad, :]


def _rglru_kernel(*refs, tiles_per_seq, ntiles):
    for slot in (0, 1):
        pl.when(lax.rem(pl.program_id(0), 2) == slot)(functools.partial(
            _rglru_step, slot, 1 - slot, *refs, tiles_per_seq=tiles_per_seq, ntiles=ntiles))


def _rglru_mixer(x, g, w_in, conv_w, conv_b, w_r, b_r, w_i, b_i, lam, w_out, *, tm):
    bsz, s, d = x.shape
    assert s % tm == 0 and tm % ROW_CHUNK == 0 and ROW_CHUNK % RG_BLOCKS == 0
    bw = d // RG_BLOCKS
    tiles_per_seq = s // tm
    ntiles = bsz * tiles_per_seq
    const2 = lambda t: (0, 0)
    const3 = lambda t: (0, 0, 0)
    vec = pl.BlockSpec((1, d), const2)
    proj_tile = lambda t: jnp.minimum(t, ntiles - 1)
    scan_tile = lambda t: jnp.maximum(t - 1, 0)
    tile_spec = lambda tile: pl.BlockSpec(
        (None, tm, d), lambda t: (tile(t) // tiles_per_seq, tile(t) % tiles_per_seq, 0))
    return pl.pallas_call(
        functools.partial(_rglru_kernel, tiles_per_seq=tiles_per_seq, ntiles=ntiles),
        grid=(ntiles + 1,),
        in_specs=[
            tile_spec(proj_tile),
            tile_spec(scan_tile),
            vec,
            pl.BlockSpec((d, 2 * d), const2),
            pl.BlockSpec((CONV_W, d), const2),
            vec,
            pl.BlockSpec((RG_BLOCKS, bw, bw), const3),
            vec,
            pl.BlockSpec((RG_BLOCKS, bw, bw), const3),
            vec,
            vec,
            pl.BlockSpec((d, d), const2),
        ],
        out_specs=tile_spec(scan_tile),
        out_shape=jax.ShapeDtypeStruct((bsz, s, d), F32),
        scratch_shapes=[
            pltpu.VMEM((tm + SUBLANES, d), F32),
            pltpu.VMEM((2, tm, d), F32),
            pltpu.VMEM((2, tm, d), F32),
            pltpu.VMEM((2, tm, d), F32),
            pltpu.VMEM((tm, d), F32),
            pltpu.VMEM((1, d), F32),
        ],
        compiler_params=pltpu.CompilerParams(
            dimension_semantics=("arbitrary",),
            vmem_limit_bytes=VMEM_LIMIT),
        name="rglru_mixer",
    )(x, x, g, w_in, conv_w, conv_b, w_r, b_r, w_i, b_i, lam, w_out)


def _ffn_kernel(*refs, pre_proj, final_norm, nchunk):
    refs = list(refs)
    x_ref = refs.pop(0)
    if pre_proj:
        att_ref = refs.pop(0)
        wo_ref = refs.pop(0)
    g_ref, wg_ref, wu_ref, wd_ref = refs[:4]
    refs = refs[4:]
    if final_norm:
        fg_ref = refs.pop(0)
    o_ref, wg_s, wu_s, wd_s, hb_s = refs

    t = pl.program_id(0)

    def start_tile():
        x = x_ref[...]
        if pre_proj:
            x = x + _dot(att_ref[...], wo_ref[...].astype(BF16))
        o_ref[...] = x
        return _rmsnorm(x, g_ref[...]).astype(BF16)

    def add_chunk(h, c):
        gate = _dot(h, wg_s[c])
        up = _dot(h, wu_s[c])
        act = (gate * jax.nn.sigmoid(gate) * up).astype(BF16)
        o_ref[...] += _dot(act, wd_s[c])

    def finish_tile():
        if final_norm:
            o_ref[...] = _rmsnorm(o_ref[...], fg_ref[...])

    @pl.when(t < nchunk)
    def _():
        wg_s[t] = wg_ref[...].astype(BF16)
        wu_s[t] = wu_ref[...].astype(BF16)
        wd_s[t] = wd_ref[...].astype(BF16)

        @pl.when(t == 0)
        def _():
            hb_s[...] = start_tile()

        add_chunk(hb_s[...], t)

        @pl.when(t == nchunk - 1)
        def _():
            finish_tile()

    @pl.when(t >= nchunk)
    def _():
        h = start_tile()
        for c in range(nchunk):
            add_chunk(h, c)
        finish_tile()


def _ffn(x, g, w_gate, w_up, w_down, layer, *, att=None, w_o=None, final_g=None, tm, tf):
    m, d = x.shape
    dff = w_gate.shape[2]
    assert dff % tf == 0 and m % tm == 0
    nchunk = dff // tf
    pre_proj = att is not None
    final_norm = final_g is not None
    resident = dict(pipeline_mode=pl.Buffered(1))
    tile_of = lambda t: jnp.maximum(t - (nchunk - 1), 0)
    chunk_of = lambda t: jnp.minimum(t, nchunk - 1)
    row = pl.BlockSpec((tm, d), lambda t: (tile_of(t), 0))
    vec = pl.BlockSpec((1, d), lambda t: (0, 0), **resident)
    args, specs = [x], [row]
    if pre_proj:
        args += [att, w_o]
        specs += [row, pl.BlockSpec((d, d), lambda t: (0, 0), **resident)]
    args += [g, w_gate, w_up, w_down]
    specs += [vec,
              pl.BlockSpec((None, d, tf), lambda t: (layer, 0, chunk_of(t))),
              pl.BlockSpec((None, d, tf), lambda t: (layer, 0, chunk_of(t))),
              pl.BlockSpec((None, tf, d), lambda t: (layer, chunk_of(t), 0))]
    if final_norm:
        args.append(final_g)
        specs.append(vec)
    return pl.pallas_call(
        functools.partial(_ffn_kernel, pre_proj=pre_proj, final_norm=final_norm, nchunk=nchunk),
        grid=(nchunk + m // tm - 1,),
        in_specs=specs,
        out_specs=row,
        out_shape=jax.ShapeDtypeStruct((m, d), F32),
        scratch_shapes=[
            pltpu.VMEM((nchunk, d, tf), BF16),
            pltpu.VMEM((nchunk, d, tf), BF16),
            pltpu.VMEM((nchunk, tf, d), BF16),
            pltpu.VMEM((tm, d), BF16),
        ],
        compiler_params=pltpu.CompilerParams(
            dimension_semantics=("arbitrary",),
            vmem_limit_bytes=VMEM_LIMIT),
        name="ffn_attn_out" if pre_proj else "ffn",
    )(*args)


def _qkv_kernel(x_ref, g_ref, w_ref, o_ref, w_s, *, q_scale, nload, pad_tiles, tiles_per_seq):
    d = x_ref.shape[1]
    t = pl.program_id(0)

    @pl.when(t < nload)
    def _():
        w_s[t] = w_ref[...].astype(BF16)

    item = jnp.maximum(t - (nload - pad_tiles), 0)
    is_pad = lax.rem(item, tiles_per_seq) < pad_tiles

    @pl.when(is_pad)
    def _():
        o_ref[...] = jnp.zeros_like(o_ref)

    @pl.when(jnp.logical_not(is_pad))
    def _():
        half = x_ref.shape[0] // 2
        for r0 in (0, half):
            h = _rmsnorm(x_ref[r0:r0 + half, :], g_ref[...]).astype(BF16)
            for c in range(nload):
                y = _dot(h, w_s[c])
                if c == 0:
                    y = y * q_scale
                o_ref[r0:r0 + half, c * d:(c + 1) * d] = y.astype(o_ref.dtype)


def _qkv_proj(x, g, w_qkv, *, tm, pad):
    bsz, s, d = x.shape
    n = w_qkv.shape[1]
    nload = n // d
    pad_tiles = pad // tm
    assert pad % tm == 0 and s % tm == 0 and 1 <= pad_tiles <= nload
    tiles_per_seq = s // tm + pad_tiles
    resident = dict(pipeline_mode=pl.Buffered(1))
    item = lambda t: jnp.maximum(t - (nload - pad_tiles), 0)
    seq_of = lambda t: item(t) // tiles_per_seq
    tile_of = lambda t: item(t) % tiles_per_seq
    return pl.pallas_call(
        functools.partial(_qkv_kernel, q_scale=LOG2E / math.sqrt(SB_HEAD_DIM), nload=nload,
                          pad_tiles=pad_tiles, tiles_per_seq=tiles_per_seq),
        grid=(nload - pad_tiles + bsz * tiles_per_seq,),
        in_specs=[
            pl.BlockSpec((None, tm, d),
                         lambda t: (seq_of(t), jnp.maximum(tile_of(t) - pad_tiles, 0), 0)),
            pl.BlockSpec((1, d), lambda t: (0, 0), **resident),
            pl.BlockSpec((d, d), lambda t: (0, jnp.minimum(t, nload - 1))),
        ],
        out_specs=pl.BlockSpec((None, tm, n), lambda t: (seq_of(t), tile_of(t), 0)),
        out_shape=jax.ShapeDtypeStruct((bsz, s + pad, n), BF16),
        scratch_shapes=[pltpu.VMEM((nload, d, d), BF16)],
        compiler_params=pltpu.CompilerParams(
            dimension_semantics=("arbitrary",),
            vmem_limit_bytes=VMEM_LIMIT),
        name="qkv_proj",
    )(x, g, w_qkv)


def _sb_attn_kernel(q_ref, k_ref, v_ref, tri_ref, bias_ref, o_ref, acc_ref, c_ref, *, tiles, pad):
    tq, tk = Q_TILE, KEY_BLOCK
    step = pl.program_id(2)

    lane = lax.broadcasted_iota(jnp.int32, (tq, LANES), 1)
    head0, head1 = lane < SB_HEAD_DIM, lane >= SB_HEAD_DIM

    def scores(q, r0, first):
        zq = jnp.zeros_like(q)
        qs = jnp.concatenate([jnp.where(head0, q, zq), jnp.where(head1, q, zq)], axis=0)
        k = k_ref[pl.ds(r0, tk), :]
        z = lax.dot_general(qs, k, (((1,), (1,)), ((), ())), preferred_element_type=F32)
        if first:
            z = jnp.concatenate([z[:, :tk - LANES], z[:, tk - LANES:] + bias_ref[...]], axis=1)
        return z

    def weights(z):
        lg = jnp.log(1.0 + jnp.exp2(jnp.minimum(z, -z))) * LOG2E
        sp = jnp.maximum(z, 0.0) + lg
        sfx = _dot(sp.astype(BF16), tri_ref[...])
        w = jnp.exp2((z - sp) + sfx)
        return w.astype(BF16), sfx, sp

    def weighted_values(w, r0, row_scale=None):
        pv = _dot(w, v_ref[pl.ds(r0, tk), :])
        if row_scale is not None:
            pv = row_scale * pv
        return jnp.where(head0, pv[:tq], pv[tq:])

    rows2 = 2 * tq
    ends = [pl.multiple_of((step * tiles + g + 1) * tq + pad, tq) for g in range(tiles)]
    w, sfx, _ = weights(jnp.concatenate(
        [scores(q_ref[g * tq:(g + 1) * tq, :], ends[g] - tk, True) for g in range(tiles)],
        axis=0))
    for g in range(tiles):
        acc = weighted_values(w[g * rows2:(g + 1) * rows2], ends[g] - tk)
        o_ref[g * tq:(g + 1) * tq, :] = acc.astype(o_ref.dtype)

    bound = jnp.max(sfx[:, :LANES], axis=0, keepdims=True)[0:1, 0:1]

    @pl.when(bound[0, 0] > EXIT_LOG)
    def _():
        def more(carry):
            end, cm = carry
            return (end > pad) & (cm > EXIT_LOG)

        def walk(g, _):
            q0 = pl.multiple_of(g * tq, tq)
            q = q_ref[pl.ds(q0, tq), :]
            first_start = (step * tiles + g + 1) * tq + pad - tk
            w, sfx, sp = weights(scores(q, pl.multiple_of(first_start, tq), True))
            acc_ref[...] = weighted_values(w, pl.multiple_of(first_start, tq))
            c = sfx[:, 0:1] - sp[:, 0:1]
            c_ref[...] = c

            def body(carry):
                end, _ = carry
                r0 = pl.multiple_of(end - tk, tq)
                w, sfx, sp = weights(scores(q, r0, False))
                c = c_ref[...]
                acc_ref[...] += weighted_values(w, r0, jnp.exp2(c))
                c = c + (sfx[:, 0:1] - sp[:, 0:1])
                c_ref[...] = c
                return end - tk, jnp.max(c)

            lax.while_loop(more, body, (first_start, jnp.max(c)))
            o_ref[pl.ds(q0, tq), :] = acc_ref[...].astype(o_ref.dtype)
            return 0

        lax.fori_loop(0, tiles, walk, 0)


def _sb_attention(qkv, *, d, pad, tiles):
    bsz, s_pad, _ = qkv.shape
    s = s_pad - pad
    pairs = d // LANES
    rows = tiles * Q_TILE
    assert pad % rows == 0 and s % rows == 0
    tq, tk = Q_TILE, KEY_BLOCK
    kr = lax.broadcasted_iota(jnp.int32, (tk, tk), 0)
    kc = lax.broadcasted_iota(jnp.int32, (tk, tk), 1)
    tri = jnp.where(kr > kc, -1.0, 0.0).astype(BF16)
    assert tq <= LANES and tk % LANES == 0
    mr = lax.broadcasted_iota(jnp.int32, (2 * tq, LANES), 0) & (tq - 1)
    mc = lax.broadcasted_iota(jnp.int32, (2 * tq, LANES), 1)
    bias = jnp.where(mc - (LANES - tq) < mr, 0.0, MASKED).astype(F32)
    const = lambda b, p, i: (0, 0)
    return pl.pallas_call(
        functools.partial(_sb_attn_kernel, tiles=tiles, pad=pad),
        grid=(bsz, pairs, s // rows),
        in_specs=[
            pl.BlockSpec((None, rows, LANES), lambda b, p, i: (b, i + pad // rows, p)),
            pl.BlockSpec((None, s_pad, LANES), lambda b, p, i: (b, 0, pairs + p)),
            pl.BlockSpec((None, s_pad, LANES), lambda b, p, i: (b, 0, 2 * pairs + p)),
            pl.BlockSpec((tk, tk), const),
            pl.BlockSpec((2 * tq, LANES), const),
        ],
        out_specs=pl.BlockSpec((None, rows, LANES), lambda b, p, i: (b, i, p)),
        out_shape=jax.ShapeDtypeStruct((bsz, s, d), BF16),
        scratch_shapes=[
            pltpu.VMEM((Q_TILE, LANES), F32),
            pltpu.VMEM((2 * Q_TILE, 1), F32),
        ],
        compiler_params=pltpu.CompilerParams(
            dimension_semantics=("arbitrary", "arbitrary", "arbitrary"),
            vmem_limit_bytes=VMEM_LIMIT),
        name="sb_attention",
    )(qkv, qkv, qkv, tri, bias)


def kernel(x, norm_mix_g, norm_ffn_g, a_w_in, a_conv_w, a_conv_b, a_w_r, a_b_r, a_w_i, a_b_i,
           a_lambda, a_w_out, b_w_qkv, b_w_out, ffn_w_gate, ffn_w_up, ffn_w_down, final_g):
    bsz, s, d = x.shape
    m = bsz * s
    row = lambda v: v.reshape(1, -1)
    bf = lambda w: w.astype(BF16)

    x = _rglru_mixer(x, row(norm_mix_g[0]), bf(a_w_in[0]), a_conv_w[0], row(a_conv_b[0]),
                     bf(0.5 * a_w_r[0]), row(0.5 * a_b_r[0]), bf(0.5 * a_w_i[0]),
                     row(0.5 * a_b_i[0]), row(a_lambda[0]), bf(a_w_out[0]), tm=RGLRU_TM)
    x = _ffn(x.reshape(m, d), row(norm_ffn_g[0]), ffn_w_gate, ffn_w_up, ffn_w_down, 0,
             tm=FFN_TM, tf=FFN_TF)

    pad = ATTN_TILES * Q_TILE
    qkv = _qkv_proj(x.reshape(bsz, s, d), row(norm_mix_g[1]), b_w_qkv[0], tm=QKV_TM, pad=pad)
    att = _sb_attention(qkv, d=d, pad=pad, tiles=ATTN_TILES)
    out = _ffn(x, row(norm_ffn_g[1]), ffn_w_gate, ffn_w_up, ffn_w_down, 1,
               att=att.reshape(m, d), w_o=b_w_out[0], final_g=row(final_g), tm=FFN_TM, tf=FFN_TF)
    return out.reshape(bsz, s, d)
```

```python
import functools
import math

import jax
import jax.numpy as jnp
from jax import lax
from jax.experimental import pallas as pl
from jax.experimental.pallas import tpu as pltpu

RMS_EPS = 1e-6
RG_C = 8.0
CONV_W = 4
RG_BLOCKS = 4
SB_HEAD_DIM = 64
LANES = 128
SUBLANES = 8
ROW_CHUNK = 256
Q_TILE = 64
KEY_BLOCK = 256
LOG2E = 1.4426950408889634
GELU_C0 = math.sqrt(2.0 / math.pi)
EXIT_LOG = -130.0
MASKED = -1e30
VMEM_LIMIT = 56 * 1024 * 1024

RGLRU_TM = 512
FFN_TM = 1024
FFN_TF = 256
QKV_TM = 1024
ATTN_TILES = 32

BF16 = jnp.bfloat16
F32 = jnp.float32


def _rmsnorm(x, g):
    ms = jnp.mean(x * x, axis=-1, keepdims=True)
    return x * lax.rsqrt(ms + RMS_EPS) * g


def _dot(a, b):
    return jnp.dot(a, b, preferred_element_type=F32)


def _gelu_tanh(x):
    inner = x * (GELU_C0 + (GELU_C0 * 0.044715) * (x * x))
    hx = 0.5 * x
    return hx + hx * jnp.tanh(inner)


def _softplus(z):
    return jnp.maximum(z, 0.0) + jnp.log(1.0 + jnp.exp(-jnp.abs(z)))


def _rglru_kernel(x_ref, g_ref, win_ref, cw_ref, cb_ref, wr_ref, br_ref, wi_ref, bi_ref,
                  lam_ref, wout_ref, o_ref, xpad_ref, a_ref, u_ref, hs_ref, h_ref):
    tm, d = x_ref.shape
    bw = d // RG_BLOCKS
    pad = SUBLANES

    @pl.when(pl.program_id(1) == 0)
    def _():
        xpad_ref[0:pad, :] = jnp.zeros((pad, d), F32)
        h_ref[...] = jnp.zeros_like(h_ref)

    half_rate = (0.5 * RG_C) * -_softplus(-lam_ref[...])

    gates = []
    h = h_ref[...]
    for c0 in range(0, tm, ROW_CHUNK):
        rows = slice(c0, c0 + ROW_CHUNK)
        hb = _rmsnorm(x_ref[rows, :], g_ref[...]).astype(BF16)
        xpad_ref[pad + c0:pad + c0 + ROW_CHUNK, :] = _dot(hb, win_ref[:, d:])

        xc = cb_ref[...] + cw_ref[CONV_W - 1:CONV_W, :] * xpad_ref[pad + c0:pad + c0 + ROW_CHUNK, :]
        for k in range(CONV_W - 1):
            off = pad + c0 - (CONV_W - 1) + k
            xc = xc + cw_ref[k:k + 1, :] * xpad_ref[off:off + ROW_CHUNK, :]

        xcb = xc.astype(BF16)
        for n in range(RG_BLOCKS):
            sl = slice(n * bw, (n + 1) * bw)
            tr = jnp.tanh(_dot(xcb[:, sl], wr_ref[n]) + br_ref[:, sl])
            ti = jnp.tanh(_dot(xcb[:, sl], wi_ref[n]) + bi_ref[:, sl])
            log_a = tr * half_rate[:, sl] + half_rate[:, sl]
            a = jnp.exp(log_a)
            mult = jnp.exp2(jnp.log(jnp.maximum(jnp.tanh(log_a) * (-1.0 - a * a), 0.0))
                            * (0.5 * LOG2E))
            a_ref[rows, sl] = a
            u_ref[rows, sl] = (mult * xc[:, sl]) * (0.5 * ti + 0.5)

        for t in range(c0, c0 + ROW_CHUNK):
            h = a_ref[t:t + 1, :] * h + u_ref[t:t + 1, :]
            hs_ref[t:t + 1, :] = h

        gates.append(_gelu_tanh(_dot(hb, win_ref[:, :d])))
    h_ref[...] = h
    xpad_ref[0:pad, :] = xpad_ref[tm:tm + pad, :]

    for c, c0 in enumerate(range(0, tm, ROW_CHUNK)):
        rows = slice(c0, c0 + ROW_CHUNK)
        y = (hs_ref[rows, :] * gates[c]).astype(BF16)
        o_ref[rows, :] = x_ref[rows, :] + _dot(y, wout_ref[...])


def _rglru_mixer(x, g, w_in, conv_w, conv_b, w_r, b_r, w_i, b_i, lam, w_out, *, tm):
    bsz, s, d = x.shape
    assert s % tm == 0 and tm % ROW_CHUNK == 0
    bw = d // RG_BLOCKS
    const2 = lambda b, i: (0, 0)
    const3 = lambda b, i: (0, 0, 0)
    vec = pl.BlockSpec((1, d), const2)
    return pl.pallas_call(
        _rglru_kernel,
        grid=(bsz, s // tm),
        in_specs=[
            pl.BlockSpec((None, tm, d), lambda b, i: (b, i, 0)),
            vec,
            pl.BlockSpec((d, 2 * d), const2),
            pl.BlockSpec((CONV_W, d), const2),
            vec,
            pl.BlockSpec((RG_BLOCKS, bw, bw), const3),
            vec,
            pl.BlockSpec((RG_BLOCKS, bw, bw), const3),
            vec,
            vec,
            pl.BlockSpec((d, d), const2),
        ],
        out_specs=pl.BlockSpec((None, tm, d), lambda b, i: (b, i, 0)),
        out_shape=jax.ShapeDtypeStruct((bsz, s, d), F32),
        scratch_shapes=[
            pltpu.VMEM((tm + SUBLANES, d), F32),
            pltpu.VMEM((tm, d), F32),
            pltpu.VMEM((tm, d), F32),
            pltpu.VMEM((tm, d), F32),
            pltpu.VMEM((1, d), F32),
        ],
        compiler_params=pltpu.CompilerParams(
            dimension_semantics=("arbitrary", "arbitrary"),
            vmem_limit_bytes=VMEM_LIMIT),
        name="rglru_mixer",
    )(x, g, w_in, conv_w, conv_b, w_r, b_r, w_i, b_i, lam, w_out)


def _ffn_kernel(*refs, pre_proj, final_norm, nchunk):
    refs = list(refs)
    x_ref = refs.pop(0)
    if pre_proj:
        att_ref = refs.pop(0)
        wo_ref = refs.pop(0)
    g_ref, wg_ref, wu_ref, wd_ref = refs[:4]
    refs = refs[4:]
    if final_norm:
        fg_ref = refs.pop(0)
    o_ref, wg_s, wu_s, wd_s, hb_s = refs

    t = pl.program_id(0)

    def start_tile():
        x = x_ref[...]
        if pre_proj:
            x = x + _dot(att_ref[...], wo_ref[...].astype(BF16))
        o_ref[...] = x
        return _rmsnorm(x, g_ref[...]).astype(BF16)

    def add_chunk(h, c):
        gate = _dot(h, wg_s[c])
        up = _dot(h, wu_s[c])
        act = (gate * jax.nn.sigmoid(gate) * up).astype(BF16)
        o_ref[...] += _dot(act, wd_s[c])

    def finish_tile():
        if final_norm:
            o_ref[...] = _rmsnorm(o_ref[...], fg_ref[...])

    @pl.when(t < nchunk)
    def _():
        wg_s[t] = wg_ref[...].astype(BF16)
        wu_s[t] = wu_ref[...].astype(BF16)
        wd_s[t] = wd_ref[...].astype(BF16)

        @pl.when(t == 0)
        def _():
            hb_s[...] = start_tile()

        add_chunk(hb_s[...], t)

        @pl.when(t == nchunk - 1)
        def _():
            finish_tile()

    @pl.when(t >= nchunk)
    def _():
        h = start_tile()
        for c in range(nchunk):
            add_chunk(h, c)
        finish_tile()


def _ffn(x, g, w_gate, w_up, w_down, layer, *, att=None, w_o=None, final_g=None, tm, tf):
    m, d = x.shape
    dff = w_gate.shape[2]
    assert dff % tf == 0 and m % tm == 0
    nchunk = dff // tf
    pre_proj = att is not None
    final_norm = final_g is not None
    resident = dict(pipeline_mode=pl.Buffered(1))
    tile_of = lambda t: jnp.maximum(t - (nchunk - 1), 0)
    chunk_of = lambda t: jnp.minimum(t, nchunk - 1)
    row = pl.BlockSpec((tm, d), lambda t: (tile_of(t), 0))
    vec = pl.BlockSpec((1, d), lambda t: (0, 0), **resident)
    args, specs = [x], [row]
    if pre_proj:
        args += [att, w_o]
        specs += [row, pl.BlockSpec((d, d), lambda t: (0, 0), **resident)]
    args += [g, w_gate, w_up, w_down]
    specs += [vec,
              pl.BlockSpec((None, d, tf), lambda t: (layer, 0, chunk_of(t))),
              pl.BlockSpec((None, d, tf), lambda t: (layer, 0, chunk_of(t))),
              pl.BlockSpec((None, tf, d), lambda t: (layer, chunk_of(t), 0))]
    if final_norm:
        args.append(final_g)
        specs.append(vec)
    return pl.pallas_call(
        functools.partial(_ffn_kernel, pre_proj=pre_proj, final_norm=final_norm, nchunk=nchunk),
        grid=(nchunk + m // tm - 1,),
        in_specs=specs,
        out_specs=row,
        out_shape=jax.ShapeDtypeStruct((m, d), F32),
        scratch_shapes=[
            pltpu.VMEM((nchunk, d, tf), BF16),
            pltpu.VMEM((nchunk, d, tf), BF16),
            pltpu.VMEM((nchunk, tf, d), BF16),
            pltpu.VMEM((tm, d), BF16),
        ],
        compiler_params=pltpu.CompilerParams(
            dimension_semantics=("arbitrary",),
            vmem_limit_bytes=VMEM_LIMIT),
        name="ffn_attn_out" if pre_proj else "ffn",
    )(*args)


def _qkv_kernel(x_ref, g_ref, w_ref, o_ref, w_s, *, q_scale, nload, pad_tiles, tiles_per_seq):
    d = x_ref.shape[1]
    t = pl.program_id(0)

    @pl.when(t < nload)
    def _():
        w_s[t] = w_ref[...].astype(BF16)

    item = jnp.maximum(t - (nload - pad_tiles), 0)
    is_pad = lax.rem(item, tiles_per_seq) < pad_tiles

    @pl.when(is_pad)
    def _():
        o_ref[...] = jnp.zeros_like(o_ref)

    @pl.when(jnp.logical_not(is_pad))
    def _():
        half = x_ref.shape[0] // 2
        for r0 in (0, half):
            h = _rmsnorm(x_ref[r0:r0 + half, :], g_ref[...]).astype(BF16)
            for c in range(nload):
                y = _dot(h, w_s[c])
                if c == 0:
                    y = y * q_scale
                o_ref[r0:r0 + half, c * d:(c + 1) * d] = y.astype(o_ref.dtype)


def _qkv_proj(x, g, w_qkv, *, tm, pad):
    bsz, s, d = x.shape
    n = w_qkv.shape[1]
    nload = n // d
    pad_tiles = pad // tm
    assert pad % tm == 0 and s % tm == 0 and 1 <= pad_tiles <= nload
    tiles_per_seq = s // tm + pad_tiles
    resident = dict(pipeline_mode=pl.Buffered(1))
    item = lambda t: jnp.maximum(t - (nload - pad_tiles), 0)
    seq_of = lambda t: item(t) // tiles_per_seq
    tile_of = lambda t: item(t) % tiles_per_seq
    return pl.pallas_call(
        functools.partial(_qkv_kernel, q_scale=LOG2E / math.sqrt(SB_HEAD_DIM), nload=nload,
                          pad_tiles=pad_tiles, tiles_per_seq=tiles_per_seq),
        grid=(nload - pad_tiles + bsz * tiles_per_seq,),
        in_specs=[
            pl.BlockSpec((None, tm, d),
                         lambda t: (seq_of(t), jnp.maximum(tile_of(t) - pad_tiles, 0), 0)),
            pl.BlockSpec((1, d), lambda t: (0, 0), **resident),
            pl.BlockSpec((d, d), lambda t: (0, jnp.minimum(t, nload - 1))),
        ],
        out_specs=pl.BlockSpec((None, tm, n), lambda t: (seq_of(t), tile_of(t), 0)),
        out_shape=jax.ShapeDtypeStruct((bsz, s + pad, n), BF16),
        scratch_shapes=[pltpu.VMEM((nload, d, d), BF16)],
        compiler_params=pltpu.CompilerParams(
            dimension_semantics=("arbitrary",),
            vmem_limit_bytes=VMEM_LIMIT),
        name="qkv_proj",
    )(x, g, w_qkv)


def _sb_attn_kernel(q_ref, k_ref, v_ref, tri_ref, bias_ref, o_ref, acc_ref, c_ref, *, tiles, pad):
    tq, tk = Q_TILE, KEY_BLOCK
    step = pl.program_id(2)

    lane = lax.broadcasted_iota(jnp.int32, (tq, LANES), 1)
    head0, head1 = lane < SB_HEAD_DIM, lane >= SB_HEAD_DIM

    def scores(q, r0, first):
        zq = jnp.zeros_like(q)
        qs = jnp.concatenate([jnp.where(head0, q, zq), jnp.where(head1, q, zq)], axis=0)
        k = k_ref[pl.ds(r0, tk), :]
        z = lax.dot_general(qs, k, (((1,), (1,)), ((), ())), preferred_element_type=F32)
        if first:
            z = jnp.concatenate([z[:, :tk - LANES], z[:, tk - LANES:] + bias_ref[...]], axis=1)
        return z

    def weights(z):
        lg = jnp.log(1.0 + jnp.exp2(jnp.minimum(z, -z))) * LOG2E
        sp = jnp.maximum(z, 0.0) + lg
        sfx = _dot(sp.astype(BF16), tri_ref[...])
        w = jnp.exp2(z + sfx)
        return w.astype(BF16), sfx

    def weighted_values(w, r0, row_scale=None):
        pv = _dot(w, v_ref[pl.ds(r0, tk), :])
        if row_scale is not None:
            pv = row_scale * pv
        return jnp.where(head0, pv[:tq], pv[tq:])

    rows2 = 2 * tq
    ends = [pl.multiple_of((step * tiles + g + 1) * tq + pad, tq) for g in range(tiles)]
    w, sfx = weights(jnp.concatenate(
        [scores(q_ref[g * tq:(g + 1) * tq, :], ends[g] - tk, True) for g in range(tiles)],
        axis=0))
    for g in range(tiles):
        acc = weighted_values(w[g * rows2:(g + 1) * rows2], ends[g] - tk)
        o_ref[g * tq:(g + 1) * tq, :] = acc.astype(o_ref.dtype)

    bound = jnp.max(sfx[:, :LANES], axis=0, keepdims=True)[0:1, 0:1]

    @pl.when(bound[0, 0] > EXIT_LOG)
    def _():
        def more(carry):
            end, cm = carry
            return (end > pad) & (cm > EXIT_LOG)

        def walk(g, _):
            q0 = pl.multiple_of(g * tq, tq)
            q = q_ref[pl.ds(q0, tq), :]
            first_start = (step * tiles + g + 1) * tq + pad - tk
            w, sfx = weights(scores(q, pl.multiple_of(first_start, tq), True))
            acc_ref[...] = weighted_values(w, pl.multiple_of(first_start, tq))
            c = sfx[:, 0:1]
            c_ref[...] = c

            def body(carry):
                end, _ = carry
                r0 = pl.multiple_of(end - tk, tq)
                w, sfx = weights(scores(q, r0, False))
                c = c_ref[...]
                acc_ref[...] += weighted_values(w, r0, jnp.exp2(c))
                c = c + sfx[:, 0:1]
                c_ref[...] = c
                return end - tk, jnp.max(c)

            lax.while_loop(more, body, (first_start, jnp.max(c)))
            o_ref[pl.ds(q0, tq), :] = acc_ref[...].astype(o_ref.dtype)
            return 0

        lax.fori_loop(0, tiles, walk, 0)


def _sb_attention(qkv, *, d, pad, tiles):
    bsz, s_pad, _ = qkv.shape
    s = s_pad - pad
    pairs = d // LANES
    rows = tiles * Q_TILE
    assert pad % rows == 0 and s % rows == 0
    tq, tk = Q_TILE, KEY_BLOCK
    kr = lax.broadcasted_iota(jnp.int32, (tk, tk), 0)
    kc = lax.broadcasted_iota(jnp.int32, (tk, tk), 1)
    tri = jnp.where(kr >= kc, -1.0, 0.0).astype(BF16)
    assert tq <= LANES and tk % LANES == 0
    mr = lax.broadcasted_iota(jnp.int32, (2 * tq, LANES), 0) & (tq - 1)
    mc = lax.broadcasted_iota(jnp.int32, (2 * tq, LANES), 1)
    bias = jnp.where(mc - (LANES - tq) < mr, 0.0, MASKED).astype(F32)
    const = lambda b, p, i: (0, 0)
    return pl.pallas_call(
        functools.partial(_sb_attn_kernel, tiles=tiles, pad=pad),
        grid=(bsz, pairs, s // rows),
        in_specs=[
            pl.BlockSpec((None, rows, LANES), lambda b, p, i: (b, i + pad // rows, p)),
            pl.BlockSpec((None, s_pad, LANES), lambda b, p, i: (b, 0, pairs + p)),
            pl.BlockSpec((None, s_pad, LANES), lambda b, p, i: (b, 0, 2 * pairs + p)),
            pl.BlockSpec((tk, tk), const),
            pl.BlockSpec((2 * tq, LANES), const),
        ],
        out_specs=pl.BlockSpec((None, rows, LANES), lambda b, p, i: (b, i, p)),
        out_shape=jax.ShapeDtypeStruct((bsz, s, d), BF16),
        scratch_shapes=[
            pltpu.VMEM((Q_TILE, LANES), F32),
            pltpu.VMEM((2 * Q_TILE, 1), F32),
        ],
        compiler_params=pltpu.CompilerParams(
            dimension_semantics=("arbitrary", "arbitrary", "arbitrary"),
            vmem_limit_bytes=VMEM_LIMIT),
        name="sb_attention",
    )(qkv, qkv, qkv, tri, bias)


def kernel(x, norm_mix_g, norm_ffn_g, a_w_in, a_conv_w, a_conv_b, a_w_r, a_b_r, a_w_i, a_b_i,
           a_lambda, a_w_out, b_w_qkv, b_w_out, ffn_w_gate, ffn_w_up, ffn_w_down, final_g):
    bsz, s, d = x.shape
    m = bsz * s
    row = lambda v: v.reshape(1, -1)
    bf = lambda w: w.astype(BF16)

    x = _rglru_mixer(x, row(norm_mix_g[0]), bf(a_w_in[0]), a_conv_w[0], row(a_conv_b[0]),
                     bf(0.5 * a_w_r[0]), row(0.5 * a_b_r[0]), bf(0.5 * a_w_i[0]),
                     row(0.5 * a_b_i[0]), row(a_lambda[0]), bf(a_w_out[0]), tm=RGLRU_TM)
    x = _ffn(x.reshape(m, d), row(norm_ffn_g[0]), ffn_w_gate, ffn_w_up, ffn_w_down, 0,
             tm=FFN_TM, tf=FFN_TF)

    pad = ATTN_TILES * Q_TILE
    qkv = _qkv_proj(x.reshape(bsz, s, d), row(norm_mix_g[1]), b_w_qkv[0], tm=QKV_TM, pad=pad)
    att = _sb_attention(qkv, d=d, pad=pad, tiles=ATTN_TILES)
    out = _ffn(x, row(norm_ffn_g[1]), ffn_w_gate, ffn_w_up, ffn_w_down, 1,
               att=att.reshape(m, d), w_o=b_w_out[0], final_g=row(final_g), tm=FFN_TM, tf=FFN_TF)
    return out.reshape(bsz, s, d)
```

```python
import functools
import math

import jax
import jax.numpy as jnp
from jax import lax
from jax.experimental import pallas as pl
from jax.experimental.pallas import tpu as pltpu

RMS_EPS = 1e-6
RG_C = 8.0
CONV_W = 4
RG_BLOCKS = 4
SB_HEAD_DIM = 64
LANES = 128
SUBLANES = 8
ROW_CHUNK = 256
SEG_PITCH = 68
Q_TILE = 64
KEY_BLOCK = 256
LOG2E = 1.4426950408889634
GELU_C0 = math.sqrt(2.0 / math.pi)
EXIT_LOG = -130.0
MASKED = -1e30
VMEM_LIMIT = 56 * 1024 * 1024

RGLRU_TM = 512
FFN_TM = 1024
FFN_TF = 256
QKV_TM = 1024
ATTN_TILES = 32

BF16 = jnp.bfloat16
F32 = jnp.float32


def _rmsnorm(x, g):
    ms = jnp.mean(x * x, axis=-1, keepdims=True)
    return x * lax.rsqrt(ms + RMS_EPS) * g


def _dot(a, b):
    return jnp.dot(a, b, preferred_element_type=F32)


def _gelu_tanh(x):
    inner = x * (GELU_C0 + (GELU_C0 * 0.044715) * (x * x))
    hx = 0.5 * x
    return hx + hx * jnp.tanh(inner)


def _softplus(z):
    return jnp.maximum(z, 0.0) + jnp.log(1.0 + jnp.exp(-jnp.abs(z)))


def _rglru_kernel(x_ref, g_ref, win_ref, cw_ref, cb_ref, wr_ref, br_ref, wi_ref, bi_ref,
                  lam_ref, wout_ref, o_ref, xpad_ref, a_ref, u_ref, hs_ref, h_ref):
    tm, d = x_ref.shape
    bw = d // RG_BLOCKS
    pad = SUBLANES

    @pl.when(pl.program_id(1) == 0)
    def _():
        xpad_ref[0:pad, :] = jnp.zeros((pad, d), F32)
        h_ref[...] = jnp.zeros_like(h_ref)

    half_rate = (0.5 * RG_C) * -_softplus(-lam_ref[...])

    gates = []
    seg = tm // SUBLANES
    for c0 in range(0, tm, ROW_CHUNK):
        rows = slice(c0, c0 + ROW_CHUNK)
        hb = _rmsnorm(x_ref[rows, :], g_ref[...]).astype(BF16)
        xpad_ref[pad + c0:pad + c0 + ROW_CHUNK, :] = _dot(hb, win_ref[:, d:])

        xc = cb_ref[...] + cw_ref[CONV_W - 1:CONV_W, :] * xpad_ref[pad + c0:pad + c0 + ROW_CHUNK, :]
        for k in range(CONV_W - 1):
            off = pad + c0 - (CONV_W - 1) + k
            xc = xc + cw_ref[k:k + 1, :] * xpad_ref[off:off + ROW_CHUNK, :]

        xcb = xc.astype(BF16)
        for n in range(RG_BLOCKS):
            sl = slice(n * bw, (n + 1) * bw)
            tr = jnp.tanh(_dot(xcb[:, sl], wr_ref[n]) + br_ref[:, sl])
            ti = jnp.tanh(_dot(xcb[:, sl], wi_ref[n]) + bi_ref[:, sl])
            log_a = tr * half_rate[:, sl] + half_rate[:, sl]
            a = jnp.exp(log_a)
            mult = jnp.exp2(jnp.log(jnp.maximum(jnp.tanh(log_a) * (-1.0 - a * a), 0.0))
                            * (0.5 * LOG2E))
            u = (mult * xc[:, sl]) * (0.5 * ti + 0.5)
            for si in range(ROW_CHUNK // seg):
                r0 = (c0 // seg + si) * SEG_PITCH
                for j in range(bw // LANES):
                    slab = n * (bw // LANES) + j
                    a_ref[slab, r0:r0 + seg, :] = a[si * seg:(si + 1) * seg,
                                                    j * LANES:(j + 1) * LANES]
                    u_ref[slab, r0:r0 + seg, :] = u[si * seg:(si + 1) * seg,
                                                    j * LANES:(j + 1) * LANES]

        gates.append(_gelu_tanh(_dot(hb, win_ref[:, :d])))
    xpad_ref[0:pad, :] = xpad_ref[tm:tm + pad, :]

    nslab = d // LANES

    def local_scan(k, carry):
        h, decay = carry
        rows = pl.ds(k, SUBLANES, stride=SEG_PITCH)
        hs, ds = [], []
        for c in range(nslab):
            a = a_ref[c, rows, :]
            hc = a * h[c] + u_ref[c, rows, :]
            dc = a * decay[c]
            hs_ref[c, rows, :] = hc
            a_ref[c, rows, :] = dc
            hs.append(hc)
            ds.append(dc)
        return jnp.stack(hs), jnp.stack(ds)

    h_loc, decay = lax.fori_loop(
        0, seg, local_scan,
        (jnp.zeros((nslab, SUBLANES, LANES), F32), jnp.ones((nslab, SUBLANES, LANES), F32)),
        unroll=8)

    cols = []
    for c in range(nslab):
        h_in = h_ref[:, c * LANES:(c + 1) * LANES]
        parts = []
        for s in range(SUBLANES):
            sr = slice(s * SEG_PITCH, s * SEG_PITCH + seg)
            parts.append(hs_ref[c, sr, :] + a_ref[c, sr, :] * h_in)
            h_in = h_loc[c, s:s + 1, :] + decay[c, s:s + 1, :] * h_in
        h_ref[:, c * LANES:(c + 1) * LANES] = h_in
        cols.append(jnp.concatenate(parts, axis=0))
    y = (jnp.concatenate(cols, axis=1) * jnp.concatenate(gates, axis=0)).astype(BF16)
    o_ref[...] = x_ref[...] + _dot(y, wout_ref[...])


def _rglru_mixer(x, g, w_in, conv_w, conv_b, w_r, b_r, w_i, b_i, lam, w_out, *, tm):
    bsz, s, d = x.shape
    assert s % tm == 0 and tm % ROW_CHUNK == 0
    bw = d // RG_BLOCKS
    const2 = lambda b, i: (0, 0)
    const3 = lambda b, i: (0, 0, 0)
    vec = pl.BlockSpec((1, d), const2)
    return pl.pallas_call(
        _rglru_kernel,
        grid=(bsz, s // tm),
        in_specs=[
            pl.BlockSpec((None, tm, d), lambda b, i: (b, i, 0)),
            vec,
            pl.BlockSpec((d, 2 * d), const2),
            pl.BlockSpec((CONV_W, d), const2),
            vec,
            pl.BlockSpec((RG_BLOCKS, bw, bw), const3),
            vec,
            pl.BlockSpec((RG_BLOCKS, bw, bw), const3),
            vec,
            vec,
            pl.BlockSpec((d, d), const2),
        ],
        out_specs=pl.BlockSpec((None, tm, d), lambda b, i: (b, i, 0)),
        out_shape=jax.ShapeDtypeStruct((bsz, s, d), F32),
        scratch_shapes=[
            pltpu.VMEM((tm + SUBLANES, d), F32),
            pltpu.VMEM((d // LANES, SUBLANES * SEG_PITCH, LANES), F32),
            pltpu.VMEM((d // LANES, SUBLANES * SEG_PITCH, LANES), F32),
            pltpu.VMEM((d // LANES, SUBLANES * SEG_PITCH, LANES), F32),
            pltpu.VMEM((1, d), F32),
        ],
        compiler_params=pltpu.CompilerParams(
            dimension_semantics=("arbitrary", "arbitrary"),
            vmem_limit_bytes=VMEM_LIMIT),
        name="rglru_mixer",
    )(x, g, w_in, conv_w, conv_b, w_r, b_r, w_i, b_i, lam, w_out)


def _ffn_kernel(*refs, pre_proj, final_norm, nchunk):
    refs = list(refs)
    x_ref = refs.pop(0)
    if pre_proj:
        att_ref = refs.pop(0)
        wo_ref = refs.pop(0)
    g_ref, wg_ref, wu_ref, wd_ref = refs[:4]
    refs = refs[4:]
    if final_norm:
        fg_ref = refs.pop(0)
    o_ref, wg_s, wu_s, wd_s, hb_s = refs

    t = pl.program_id(0)

    def start_tile():
        x = x_ref[...]
        if pre_proj:
            x = x + _dot(att_ref[...], wo_ref[...].astype(BF16))
        o_ref[...] = x
        return _rmsnorm(x, g_ref[...]).astype(BF16)

    def add_chunk(h, c):
        gate = _dot(h, wg_s[c])
        up = _dot(h, wu_s[c])
        act = (gate * jax.nn.sigmoid(gate) * up).astype(BF16)
        o_ref[...] += _dot(act, wd_s[c])

    def finish_tile():
        if final_norm:
            o_ref[...] = _rmsnorm(o_ref[...], fg_ref[...])

    @pl.when(t < nchunk)
    def _():
        wg_s[t] = wg_ref[...].astype(BF16)
        wu_s[t] = wu_ref[...].astype(BF16)
        wd_s[t] = wd_ref[...].astype(BF16)

        @pl.when(t == 0)
        def _():
            hb_s[...] = start_tile()

        add_chunk(hb_s[...], t)

        @pl.when(t == nchunk - 1)
        def _():
            finish_tile()

    @pl.when(t >= nchunk)
    def _():
        h = start_tile()
        for c in range(nchunk):
            add_chunk(h, c)
        finish_tile()


def _ffn(x, g, w_gate, w_up, w_down, layer, *, att=None, w_o=None, final_g=None, tm, tf):
    m, d = x.shape
    dff = w_gate.shape[2]
    assert dff % tf == 0 and m % tm == 0
    nchunk = dff // tf
    pre_proj = att is not None
    final_norm = final_g is not None
    resident = dict(pipeline_mode=pl.Buffered(1))
    tile_of = lambda t: jnp.maximum(t - (nchunk - 1), 0)
    chunk_of = lambda t: jnp.minimum(t, nchunk - 1)
    row = pl.BlockSpec((tm, d), lambda t: (tile_of(t), 0))
    vec = pl.BlockSpec((1, d), lambda t: (0, 0), **resident)
    args, specs = [x], [row]
    if pre_proj:
        args += [att, w_o]
        specs += [row, pl.BlockSpec((d, d), lambda t: (0, 0), **resident)]
    args += [g, w_gate, w_up, w_down]
    specs += [vec,
              pl.BlockSpec((None, d, tf), lambda t: (layer, 0, chunk_of(t))),
              pl.BlockSpec((None, d, tf), lambda t: (layer, 0, chunk_of(t))),
              pl.BlockSpec((None, tf, d), lambda t: (layer, chunk_of(t), 0))]
    if final_norm:
        args.append(final_g)
        specs.append(vec)
    return pl.pallas_call(
        functools.partial(_ffn_kernel, pre_proj=pre_proj, final_norm=final_norm, nchunk=nchunk),
        grid=(nchunk + m // tm - 1,),
        in_specs=specs,
        out_specs=row,
        out_shape=jax.ShapeDtypeStruct((m, d), F32),
        scratch_shapes=[
            pltpu.VMEM((nchunk, d, tf), BF16),
            pltpu.VMEM((nchunk, d, tf), BF16),
            pltpu.VMEM((nchunk, tf, d), BF16),
            pltpu.VMEM((tm, d), BF16),
        ],
        compiler_params=pltpu.CompilerParams(
            dimension_semantics=("arbitrary",),
            vmem_limit_bytes=VMEM_LIMIT),
        name="ffn_attn_out" if pre_proj else "ffn",
    )(*args)


def _qkv_kernel(x_ref, g_ref, w_ref, o_ref, w_s, *, q_scale, nload, pad_tiles, tiles_per_seq):
    d = x_ref.shape[1]
    t = pl.program_id(0)

    @pl.when(t < nload)
    def _():
        w_s[t] = w_ref[...].astype(BF16)

    item = jnp.maximum(t - (nload - pad_tiles), 0)
    is_pad = lax.rem(item, tiles_per_seq) < pad_tiles

    @pl.when(is_pad)
    def _():
        o_ref[...] = jnp.zeros_like(o_ref)

    @pl.when(jnp.logical_not(is_pad))
    def _():
        half = x_ref.shape[0] // 2
        for r0 in (0, half):
            h = _rmsnorm(x_ref[r0:r0 + half, :], g_ref[...]).astype(BF16)
            for c in range(nload):
                y = _dot(h, w_s[c])
                if c == 0:
                    y = y * q_scale
                o_ref[r0:r0 + half, c * d:(c + 1) * d] = y.astype(o_ref.dtype)


def _qkv_proj(x, g, w_qkv, *, tm, pad):
    bsz, s, d = x.shape
    n = w_qkv.shape[1]
    nload = n // d
    pad_tiles = pad // tm
    assert pad % tm == 0 and s % tm == 0 and 1 <= pad_tiles <= nload
    tiles_per_seq = s // tm + pad_tiles
    resident = dict(pipeline_mode=pl.Buffered(1))
    item = lambda t: jnp.maximum(t - (nload - pad_tiles), 0)
    seq_of = lambda t: item(t) // tiles_per_seq
    tile_of = lambda t: item(t) % tiles_per_seq
    return pl.pallas_call(
        functools.partial(_qkv_kernel, q_scale=LOG2E / math.sqrt(SB_HEAD_DIM), nload=nload,
                          pad_tiles=pad_tiles, tiles_per_seq=tiles_per_seq),
        grid=(nload - pad_tiles + bsz * tiles_per_seq,),
        in_specs=[
            pl.BlockSpec((None, tm, d),
                         lambda t: (seq_of(t), jnp.maximum(tile_of(t) - pad_tiles, 0), 0)),
            pl.BlockSpec((1, d), lambda t: (0, 0), **resident),
            pl.BlockSpec((d, d), lambda t: (0, jnp.minimum(t, nload - 1))),
        ],
        out_specs=pl.BlockSpec((None, tm, n), lambda t: (seq_of(t), tile_of(t), 0)),
        out_shape=jax.ShapeDtypeStruct((bsz, s + pad, n), BF16),
        scratch_shapes=[pltpu.VMEM((nload, d, d), BF16)],
        compiler_params=pltpu.CompilerParams(
            dimension_semantics=("arbitrary",),
            vmem_limit_bytes=VMEM_LIMIT),
        name="qkv_proj",
    )(x, g, w_qkv)


def _sb_attn_kernel(q_ref, k_ref, v_ref, tri_ref, bias_ref, o_ref, acc_ref, c_ref, *, tiles, pad):
    tq, tk = Q_TILE, KEY_BLOCK
    step = pl.program_id(2)

    lane = lax.broadcasted_iota(jnp.int32, (tq, LANES), 1)
    head0, head1 = lane < SB_HEAD_DIM, lane >= SB_HEAD_DIM

    def scores(q, r0, first):
        zq = jnp.zeros_like(q)
        qs = jnp.concatenate([jnp.where(head0, q, zq), jnp.where(head1, q, zq)], axis=0)
        k = k_ref[pl.ds(r0, tk), :]
        z = lax.dot_general(qs, k, (((1,), (1,)), ((), ())), preferred_element_type=F32)
        if first:
            z = jnp.concatenate([z[:, :tk - LANES], z[:, tk - LANES:] + bias_ref[...]], axis=1)
        return z

    def weights(z):
        lg = jnp.log(1.0 + jnp.exp2(jnp.minimum(z, -z))) * LOG2E
        sp = jnp.maximum(z, 0.0) + lg
        sfx = _dot(sp.astype(BF16), tri_ref[...])
        w = jnp.exp2((z - sp) + sfx)
        return w.astype(BF16), sfx, sp

    def weighted_values(w, r0, row_scale=None):
        pv = _dot(w, v_ref[pl.ds(r0, tk), :])
        if row_scale is not None:
            pv = row_scale * pv
        return jnp.where(head0, pv[:tq], pv[tq:])

    rows2 = 2 * tq
    ends = [pl.multiple_of((step * tiles + g + 1) * tq + pad, tq) for g in range(tiles)]
    w, sfx, _ = weights(jnp.concatenate(
        [scores(q_ref[g * tq:(g + 1) * tq, :], ends[g] - tk, True) for g in range(tiles)],
        axis=0))
    for g in range(tiles):
        acc = weighted_values(w[g * rows2:(g + 1) * rows2], ends[g] - tk)
        o_ref[g * tq:(g + 1) * tq, :] = acc.astype(o_ref.dtype)

    bound = jnp.max(sfx[:, :LANES], axis=0, keepdims=True)[0:1, 0:1]

    @pl.when(bound[0, 0] > EXIT_LOG)
    def _():
        def more(carry):
            end, cm = carry
            return (end > pad) & (cm > EXIT_LOG)

        def walk(g, _):
            q0 = pl.multiple_of(g * tq, tq)
            q = q_ref[pl.ds(q0, tq), :]
            first_start = (step * tiles + g + 1) * tq + pad - tk
            w, sfx, sp = weights(scores(q, pl.multiple_of(first_start, tq), True))
            acc_ref[...] = weighted_values(w, pl.multiple_of(first_start, tq))
            c = sfx[:, 0:1] - sp[:, 0:1]
            c_ref[...] = c

            def body(carry):
                end, _ = carry
                r0 = pl.multiple_of(end - tk, tq)
                w, sfx, sp = weights(scores(q, r0, False))
                c = c_ref[...]
                acc_ref[...] += weighted_values(w, r0, jnp.exp2(c))
                c = c + (sfx[:, 0:1] - sp[:, 0:1])
                c_ref[...] = c
                return end - tk, jnp.max(c)

            lax.while_loop(more, body, (first_start, jnp.max(c)))
            o_ref[pl.ds(q0, tq), :] = acc_ref[...].astype(o_ref.dtype)
            return 0

        lax.fori_loop(0, tiles, walk, 0)


def _sb_attention(qkv, *, d, pad, tiles):
    bsz, s_pad, _ = qkv.shape
    s = s_pad - pad
    pairs = d // LANES
    rows = tiles * Q_TILE
    assert pad % rows == 0 and s % rows == 0
    tq, tk = Q_TILE, KEY_BLOCK
    kr = lax.broadcasted_iota(jnp.int32, (tk, tk), 0)
    kc = lax.broadcasted_iota(jnp.int32, (tk, tk), 1)
    tri = jnp.where(kr > kc, -1.0, 0.0).astype(BF16)
    assert tq <= LANES and tk % LANES == 0
    mr = lax.broadcasted_iota(jnp.int32, (2 * tq, LANES), 0) & (tq - 1)
    mc = lax.broadcasted_iota(jnp.int32, (2 * tq, LANES), 1)
    bias = jnp.where(mc - (LANES - tq) < mr, 0.0, MASKED).astype(F32)
    const = lambda b, p, i: (0, 0)
    return pl.pallas_call(
        functools.partial(_sb_attn_kernel, tiles=tiles, pad=pad),
        grid=(bsz, pairs, s // rows),
        in_specs=[
            pl.BlockSpec((None, rows, LANES), lambda b, p, i: (b, i + pad // rows, p)),
            pl.BlockSpec((None, s_pad, LANES), lambda b, p, i: (b, 0, pairs + p)),
            pl.BlockSpec((None, s_pad, LANES), lambda b, p, i: (b, 0, 2 * pairs + p)),
            pl.BlockSpec((tk, tk), const),
            pl.BlockSpec((2 * tq, LANES), const),
        ],
        out_specs=pl.BlockSpec((None, rows, LANES), lambda b, p, i: (b, i, p)),
        out_shape=jax.ShapeDtypeStruct((bsz, s, d), BF16),
        scratch_shapes=[
            pltpu.VMEM((Q_TILE, LANES), F32),
            pltpu.VMEM((2 * Q_TILE, 1), F32),
        ],
        compiler_params=pltpu.CompilerParams(
            dimension_semantics=("arbitrary", "arbitrary", "arbitrary"),
            vmem_limit_bytes=VMEM_LIMIT),
        name="sb_attention",
    )(qkv, qkv, qkv, tri, bias)


def kernel(x, norm_mix_g, norm_ffn_g, a_w_in, a_conv_w, a_conv_b, a_w_r, a_b_r, a_w_i, a_b_i,
           a_lambda, a_w_out, b_w_qkv, b_w_out, ffn_w_gate, ffn_w_up, ffn_w_down, final_g):
    bsz, s, d = x.shape
    m = bsz * s
    row = lambda v: v.reshape(1, -1)
    bf = lambda w: w.astype(BF16)

    x = _rglru_mixer(x, row(norm_mix_g[0]), bf(a_w_in[0]), a_conv_w[0], row(a_conv_b[0]),
                     bf(0.5 * a_w_r[0]), row(0.5 * a_b_r[0]), bf(0.5 * a_w_i[0]),
                     row(0.5 * a_b_i[0]), row(a_lambda[0]), bf(a_w_out[0]), tm=RGLRU_TM)
    x = _ffn(x.reshape(m, d), row(norm_ffn_g[0]), ffn_w_gate, ffn_w_up, ffn_w_down, 0,
             tm=FFN_TM, tf=FFN_TF)

    pad = ATTN_TILES * Q_TILE
    qkv = _qkv_proj(x.reshape(bsz, s, d), row(norm_mix_g[1]), b_w_qkv[0], tm=QKV_TM, pad=pad)
    att = _sb_attention(qkv, d=d, pad=pad, tiles=ATTN_TILES)
    out = _ffn(x, row(norm_ffn_g[1]), ffn_w_gate, ffn_w_up, ffn_w_down, 1,
               att=att.reshape(m, d), w_o=b_w_out[0], final_g=row(final_g), tm=FFN_TM, tf=FFN_TF)
    return out.reshape(bsz, s, d)
```

```python
import functools
import math

import jax
import jax.numpy as jnp
from jax import lax
from jax.experimental import pallas as pl
from jax.experimental.pallas import tpu as pltpu

RMS_EPS = 1e-6
RG_C = 8.0
CONV_W = 4
RG_BLOCKS = 4
SB_HEAD_DIM = 64
LANES = 128
SUBLANES = 8
ROW_CHUNK = 512
Q_TILE = 64
KEY_BLOCK = 256
LOG2E = 1.4426950408889634
GELU_C0 = math.sqrt(2.0 / math.pi)
EXIT_LOG = -130.0
MASKED = -1e30
VMEM_LIMIT = 56 * 1024 * 1024

RGLRU_TM = 1024
FFN_TM = 1024
FFN_TF = 256
QKV_TM = 1024
ATTN_TILES = 32

BF16 = jnp.bfloat16
F32 = jnp.float32


def _rmsnorm(x, g):
    ms = jnp.mean(x * x, axis=-1, keepdims=True)
    return x * lax.rsqrt(ms + RMS_EPS) * g


def _dot(a, b):
    return jnp.dot(a, b, preferred_element_type=F32)


def _gelu_tanh(x):
    inner = x * (GELU_C0 + (GELU_C0 * 0.044715) * (x * x))
    hx = 0.5 * x
    return hx + hx * jnp.tanh(inner)


def _softplus(z):
    return jnp.maximum(z, 0.0) + jnp.log(1.0 + jnp.exp(-jnp.abs(z)))


def _rglru_kernel(x_ref, g_ref, win_ref, cw_ref, cb_ref, wr_ref, br_ref, wi_ref, bi_ref,
                  lam_ref, wout_ref, o_ref, xpad_ref, a_ref, u_ref, hs_ref, h_ref):
    tm, d = x_ref.shape
    bw = d // RG_BLOCKS
    pad = SUBLANES

    @pl.when(pl.program_id(1) == 0)
    def _():
        xpad_ref[0:pad, :] = jnp.zeros((pad, d), F32)
        h_ref[...] = jnp.zeros_like(h_ref)

    half_rate = (0.5 * RG_C) * -_softplus(-lam_ref[...])

    gates = []
    h = h_ref[...]
    for c0 in range(0, tm, ROW_CHUNK):
        rows = slice(c0, c0 + ROW_CHUNK)
        hb = _rmsnorm(x_ref[rows, :], g_ref[...]).astype(BF16)
        xpad_ref[pad + c0:pad + c0 + ROW_CHUNK, :] = _dot(hb, win_ref[:, d:])

        xc = cb_ref[...] + cw_ref[CONV_W - 1:CONV_W, :] * xpad_ref[pad + c0:pad + c0 + ROW_CHUNK, :]
        for k in range(CONV_W - 1):
            off = pad + c0 - (CONV_W - 1) + k
            xc = xc + cw_ref[k:k + 1, :] * xpad_ref[off:off + ROW_CHUNK, :]

        xcb = xc.astype(BF16)
        for n in range(RG_BLOCKS):
            sl = slice(n * bw, (n + 1) * bw)
            tr = jnp.tanh(_dot(xcb[:, sl], wr_ref[n]) + br_ref[:, sl])
            ti = jnp.tanh(_dot(xcb[:, sl], wi_ref[n]) + bi_ref[:, sl])
            log_a = tr * half_rate[:, sl] + half_rate[:, sl]
            a = jnp.exp(log_a)
            mult = jnp.exp2(jnp.log(jnp.maximum(jnp.tanh(log_a) * (-1.0 - a * a), 0.0))
                            * (0.5 * LOG2E))
            a_ref[rows, sl] = a
            u_ref[rows, sl] = (mult * xc[:, sl]) * (0.5 * ti + 0.5)

        for t in range(c0, c0 + ROW_CHUNK):
            h = a_ref[t:t + 1, :] * h + u_ref[t:t + 1, :]
            hs_ref[t:t + 1, :] = h

        gates.append(_gelu_tanh(_dot(hb, win_ref[:, :d])))
    h_ref[...] = h
    xpad_ref[0:pad, :] = xpad_ref[tm:tm + pad, :]

    for c, c0 in enumerate(range(0, tm, ROW_CHUNK)):
        rows = slice(c0, c0 + ROW_CHUNK)
        y = (hs_ref[rows, :] * gates[c]).astype(BF16)
        o_ref[rows, :] = x_ref[rows, :] + _dot(y, wout_ref[...])


def _rglru_mixer(x, g, w_in, conv_w, conv_b, w_r, b_r, w_i, b_i, lam, w_out, *, tm):
    bsz, s, d = x.shape
    assert s % tm == 0 and tm % ROW_CHUNK == 0
    bw = d // RG_BLOCKS
    const2 = lambda b, i: (0, 0)
    const3 = lambda b, i: (0, 0, 0)
    vec = pl.BlockSpec((1, d), const2)
    return pl.pallas_call(
        _rglru_kernel,
        grid=(bsz, s // tm),
        in_specs=[
            pl.BlockSpec((None, tm, d), lambda b, i: (b, i, 0)),
            vec,
            pl.BlockSpec((d, 2 * d), const2),
            pl.BlockSpec((CONV_W, d), const2),
            vec,
            pl.BlockSpec((RG_BLOCKS, bw, bw), const3),
            vec,
            pl.BlockSpec((RG_BLOCKS, bw, bw), const3),
            vec,
            vec,
            pl.BlockSpec((d, d), const2),
        ],
        out_specs=pl.BlockSpec((None, tm, d), lambda b, i: (b, i, 0)),
        out_shape=jax.ShapeDtypeStruct((bsz, s, d), F32),
        scratch_shapes=[
            pltpu.VMEM((tm + SUBLANES, d), F32),
            pltpu.VMEM((tm, d), F32),
            pltpu.VMEM((tm, d), F32),
            pltpu.VMEM((tm, d), F32),
            pltpu.VMEM((1, d), F32),
        ],
        compiler_params=pltpu.CompilerParams(
            dimension_semantics=("arbitrary", "arbitrary"),
            vmem_limit_bytes=VMEM_LIMIT),
        name="rglru_mixer",
    )(x, g, w_in, conv_w, conv_b, w_r, b_r, w_i, b_i, lam, w_out)


def _ffn_kernel(*refs, pre_proj, final_norm, nchunk):
    refs = list(refs)
    x_ref = refs.pop(0)
    if pre_proj:
        att_ref = refs.pop(0)
        wo_ref = refs.pop(0)
    g_ref, wg_ref, wu_ref, wd_ref = refs[:4]
    refs = refs[4:]
    if final_norm:
        fg_ref = refs.pop(0)
    o_ref, wg_s, wu_s, wd_s, hb_s = refs

    t = pl.program_id(0)

    def start_tile():
        x = x_ref[...]
        if pre_proj:
            x = x + _dot(att_ref[...], wo_ref[...].astype(BF16))
        o_ref[...] = x
        return _rmsnorm(x, g_ref[...]).astype(BF16)

    def add_chunk(h, c):
        gate = _dot(h, wg_s[c])
        up = _dot(h, wu_s[c])
        act = (gate * jax.nn.sigmoid(gate) * up).astype(BF16)
        o_ref[...] += _dot(act, wd_s[c])

    def finish_tile():
        if final_norm:
            o_ref[...] = _rmsnorm(o_ref[...], fg_ref[...])

    @pl.when(t < nchunk)
    def _():
        wg_s[t] = wg_ref[...].astype(BF16)
        wu_s[t] = wu_ref[...].astype(BF16)
        wd_s[t] = wd_ref[...].astype(BF16)

        @pl.when(t == 0)
        def _():
            hb_s[...] = start_tile()

        add_chunk(hb_s[...], t)

        @pl.when(t == nchunk - 1)
        def _():
            finish_tile()

    @pl.when(t >= nchunk)
    def _():
        h = start_tile()
        for c in range(nchunk):
            add_chunk(h, c)
        finish_tile()


def _ffn(x, g, w_gate, w_up, w_down, layer, *, att=None, w_o=None, final_g=None, tm, tf):
    m, d = x.shape
    dff = w_gate.shape[2]
    assert dff % tf == 0 and m % tm == 0
    nchunk = dff // tf
    pre_proj = att is not None
    final_norm = final_g is not None
    resident = dict(pipeline_mode=pl.Buffered(1))
    tile_of = lambda t: jnp.maximum(t - (nchunk - 1), 0)
    chunk_of = lambda t: jnp.minimum(t, nchunk - 1)
    row = pl.BlockSpec((tm, d), lambda t: (tile_of(t), 0))
    vec = pl.BlockSpec((1, d), lambda t: (0, 0), **resident)
    args, specs = [x], [row]
    if pre_proj:
        args += [att, w_o]
        specs += [row, pl.BlockSpec((d, d), lambda t: (0, 0), **resident)]
    args += [g, w_gate, w_up, w_down]
    specs += [vec,
              pl.BlockSpec((None, d, tf), lambda t: (layer, 0, chunk_of(t))),
              pl.BlockSpec((None, d, tf), lambda t: (layer, 0, chunk_of(t))),
              pl.BlockSpec((None, tf, d), lambda t: (layer, chunk_of(t), 0))]
    if final_norm:
        args.append(final_g)
        specs.append(vec)
    return pl.pallas_call(
        functools.partial(_ffn_kernel, pre_proj=pre_proj, final_norm=final_norm, nchunk=nchunk),
        grid=(nchunk + m // tm - 1,),
        in_specs=specs,
        out_specs=row,
        out_shape=jax.ShapeDtypeStruct((m, d), F32),
        scratch_shapes=[
            pltpu.VMEM((nchunk, d, tf), BF16),
            pltpu.VMEM((nchunk, d, tf), BF16),
            pltpu.VMEM((nchunk, tf, d), BF16),
            pltpu.VMEM((tm, d), BF16),
        ],
        compiler_params=pltpu.CompilerParams(
            dimension_semantics=("arbitrary",),
            vmem_limit_bytes=VMEM_LIMIT),
        name="ffn_attn_out" if pre_proj else "ffn",
    )(*args)


def _qkv_kernel(x_ref, g_ref, w_ref, o_ref, w_s, *, q_scale, nload, pad_tiles, tiles_per_seq):
    d = x_ref.shape[1]
    t = pl.program_id(0)

    @pl.when(t < nload)
    def _():
        w_s[t] = w_ref[...].astype(BF16)

    item = jnp.maximum(t - (nload - pad_tiles), 0)
    is_pad = lax.rem(item, tiles_per_seq) < pad_tiles

    @pl.when(is_pad)
    def _():
        o_ref[...] = jnp.zeros_like(o_ref)

    @pl.when(jnp.logical_not(is_pad))
    def _():
        half = x_ref.shape[0] // 2
        for r0 in (0, half):
            h = _rmsnorm(x_ref[r0:r0 + half, :], g_ref[...]).astype(BF16)
            for c in range(nload):
                y = _dot(h, w_s[c])
                if c == 0:
                    y = y * q_scale
                o_ref[r0:r0 + half, c * d:(c + 1) * d] = y.astype(o_ref.dtype)


def _qkv_proj(x, g, w_qkv, *, tm, pad):
    bsz, s, d = x.shape
    n = w_qkv.shape[1]
    nload = n // d
    pad_tiles = pad // tm
    assert pad % tm == 0 and s % tm == 0 and 1 <= pad_tiles <= nload
    tiles_per_seq = s // tm + pad_tiles
    resident = dict(pipeline_mode=pl.Buffered(1))
    item = lambda t: jnp.maximum(t - (nload - pad_tiles), 0)
    seq_of = lambda t: item(t) // tiles_per_seq
    tile_of = lambda t: item(t) % tiles_per_seq
    return pl.pallas_call(
        functools.partial(_qkv_kernel, q_scale=LOG2E / math.sqrt(SB_HEAD_DIM), nload=nload,
                          pad_tiles=pad_tiles, tiles_per_seq=tiles_per_seq),
        grid=(nload - pad_tiles + bsz * tiles_per_seq,),
        in_specs=[
            pl.BlockSpec((None, tm, d),
                         lambda t: (seq_of(t), jnp.maximum(tile_of(t) - pad_tiles, 0), 0)),
            pl.BlockSpec((1, d), lambda t: (0, 0), **resident),
            pl.BlockSpec((d, d), lambda t: (0, jnp.minimum(t, nload - 1))),
        ],
        out_specs=pl.BlockSpec((None, tm, n), lambda t: (seq_of(t), tile_of(t), 0)),
        out_shape=jax.ShapeDtypeStruct((bsz, s + pad, n), BF16),
        scratch_shapes=[pltpu.VMEM((nload, d, d), BF16)],
        compiler_params=pltpu.CompilerParams(
            dimension_semantics=("arbitrary",),
            vmem_limit_bytes=VMEM_LIMIT),
        name="qkv_proj",
    )(x, g, w_qkv)


def _sb_attn_kernel(q_ref, k_ref, v_ref, tri_ref, bias_ref, o_ref, acc_ref, c_ref, *, tiles, pad):
    tq, tk = Q_TILE, KEY_BLOCK
    step = pl.program_id(2)

    lane = lax.broadcasted_iota(jnp.int32, (tq, LANES), 1)
    head0, head1 = lane < SB_HEAD_DIM, lane >= SB_HEAD_DIM

    def scores(q, r0, first):
        zq = jnp.zeros_like(q)
        qs = jnp.concatenate([jnp.where(head0, q, zq), jnp.where(head1, q, zq)], axis=0)
        k = k_ref[pl.ds(r0, tk), :]
        z = lax.dot_general(qs, k, (((1,), (1,)), ((), ())), preferred_element_type=F32)
        if first:
            z = jnp.concatenate([z[:, :tk - LANES], z[:, tk - LANES:] + bias_ref[...]], axis=1)
        return z

    def weights(z):
        lg = jnp.log(1.0 + jnp.exp2(jnp.minimum(z, -z))) * LOG2E
        sp = jnp.maximum(z, 0.0) + lg
        sfx = _dot(sp.astype(BF16), tri_ref[...])
        w = jnp.exp2((z - sp) + sfx)
        return w.astype(BF16), sfx, sp

    def weighted_values(w, r0, row_scale=None):
        pv = _dot(w, v_ref[pl.ds(r0, tk), :])
        if row_scale is not None:
            pv = row_scale * pv
        return jnp.where(head0, pv[:tq], pv[tq:])

    rows2 = 2 * tq
    ends = [pl.multiple_of((step * tiles + g + 1) * tq + pad, tq) for g in range(tiles)]
    w, sfx, _ = weights(jnp.concatenate(
        [scores(q_ref[g * tq:(g + 1) * tq, :], ends[g] - tk, True) for g in range(tiles)],
        axis=0))
    for g in range(tiles):
        acc = weighted_values(w[g * rows2:(g + 1) * rows2], ends[g] - tk)
        o_ref[g * tq:(g + 1) * tq, :] = acc.astype(o_ref.dtype)

    bound = jnp.max(sfx[:, :LANES], axis=0, keepdims=True)[0:1, 0:1]

    @pl.when(bound[0, 0] > EXIT_LOG)
    def _():
        def more(carry):
            end, cm = carry
            return (end > pad) & (cm > EXIT_LOG)

        def walk(g, _):
            q0 = pl.multiple_of(g * tq, tq)
            q = q_ref[pl.ds(q0, tq), :]
            first_start = (step * tiles + g + 1) * tq + pad - tk
            w, sfx, sp = weights(scores(q, pl.multiple_of(first_start, tq), True))
            acc_ref[...] = weighted_values(w, pl.multiple_of(first_start, tq))
            c = sfx[:, 0:1] - sp[:, 0:1]
            c_ref[...] = c

            def body(carry):
                end, _ = carry
                r0 = pl.multiple_of(end - tk, tq)
                w, sfx, sp = weights(scores(q, r0, False))
                c = c_ref[...]
                acc_ref[...] += weighted_values(w, r0, jnp.exp2(c))
                c = c + (sfx[:, 0:1] - sp[:, 0:1])
                c_ref[...] = c
                return end - tk, jnp.max(c)

            lax.while_loop(more, body, (first_start, jnp.max(c)))
            o_ref[pl.ds(q0, tq), :] = acc_ref[...].astype(o_ref.dtype)
            return 0

        lax.fori_loop(0, tiles, walk, 0)


def _sb_attention(qkv, *, d, pad, tiles):
    bsz, s_pad, _ = qkv.shape
    s = s_pad - pad
    pairs = d // LANES
    rows = tiles * Q_TILE
    assert pad % rows == 0 and s % rows == 0
    tq, tk = Q_TILE, KEY_BLOCK
    kr = lax.broadcasted_iota(jnp.int32, (tk, tk), 0)
    kc = lax.broadcasted_iota(jnp.int32, (tk, tk), 1)
    tri = jnp.where(kr > kc, -1.0, 0.0).astype(BF16)
    assert tq <= LANES and tk % LANES == 0
    mr = lax.broadcasted_iota(jnp.int32, (2 * tq, LANES), 0) & (tq - 1)
    mc = lax.broadcasted_iota(jnp.int32, (2 * tq, LANES), 1)
    bias = jnp.where(mc - (LANES - tq) < mr, 0.0, MASKED).astype(F32)
    const = lambda b, p, i: (0, 0)
    return pl.pallas_call(
        functools.partial(_sb_attn_kernel, tiles=tiles, pad=pad),
        grid=(bsz, pairs, s // rows),
        in_specs=[
            pl.BlockSpec((None, rows, LANES), lambda b, p, i: (b, i + pad // rows, p)),
            pl.BlockSpec((None, s_pad, LANES), lambda b, p, i: (b, 0, pairs + p)),
            pl.BlockSpec((None, s_pad, LANES), lambda b, p, i: (b, 0, 2 * pairs + p)),
            pl.BlockSpec((tk, tk), const),
            pl.BlockSpec((2 * tq, LANES), const),
        ],
        out_specs=pl.BlockSpec((None, rows, LANES), lambda b, p, i: (b, i, p)),
        out_shape=jax.ShapeDtypeStruct((bsz, s, d), BF16),
        scratch_shapes=[
            pltpu.VMEM((Q_TILE, LANES), F32),
            pltpu.VMEM((2 * Q_TILE, 1), F32),
        ],
        compiler_params=pltpu.CompilerParams(
            dimension_semantics=("arbitrary", "arbitrary", "arbitrary"),
            vmem_limit_bytes=VMEM_LIMIT),
        name="sb_attention",
    )(qkv, qkv, qkv, tri, bias)


def kernel(x, norm_mix_g, norm_ffn_g, a_w_in, a_conv_w, a_conv_b, a_w_r, a_b_r, a_w_i, a_b_i,
           a_lambda, a_w_out, b_w_qkv, b_w_out, ffn_w_gate, ffn_w_up, ffn_w_down, final_g):
    bsz, s, d = x.shape
    m = bsz * s
    row = lambda v: v.reshape(1, -1)
    bf = lambda w: w.astype(BF16)

    x = _rglru_mixer(x, row(norm_mix_g[0]), bf(a_w_in[0]), a_conv_w[0], row(a_conv_b[0]),
                     bf(0.5 * a_w_r[0]), row(0.5 * a_b_r[0]), bf(0.5 * a_w_i[0]),
                     row(0.5 * a_b_i[0]), row(a_lambda[0]), bf(a_w_out[0]), tm=RGLRU_TM)
    x = _ffn(x.reshape(m, d), row(norm_ffn_g[0]), ffn_w_gate, ffn_w_up, ffn_w_down, 0,
             tm=FFN_TM, tf=FFN_TF)

    pad = ATTN_TILES * Q_TILE
    qkv = _qkv_proj(x.reshape(bsz, s, d), row(norm_mix_g[1]), b_w_qkv[0], tm=QKV_TM, pad=pad)
    att = _sb_attention(qkv, d=d, pad=pad, tiles=ATTN_TILES)
    out = _ffn(x, row(norm_ffn_g[1]), ffn_w_gate, ffn_w_up, ffn_w_down, 1,
               att=att.reshape(m, d), w_o=b_w_out[0], final_g=row(final_g), tm=FFN_TM, tf=FFN_TF)
    return out.reshape(bsz, s, d)
```

```python
import functools
import math

import jax
import jax.numpy as jnp
from jax import lax
from jax.experimental import pallas as pl
from jax.experimental.pallas import tpu as pltpu

RMS_EPS = 1e-6
RG_C = 8.0
CONV_W = 4
RG_BLOCKS = 4
SB_HEAD_DIM = 64
LANES = 128
SUBLANES = 8
ROW_CHUNK = 512
Q_TILE = 64
KEY_BLOCK = 256
LOG2E = 1.4426950408889634
GELU_C0 = math.sqrt(2.0 / math.pi)
EXIT_LOG = -130.0
MASKED = -1e30
VMEM_LIMIT = 56 * 1024 * 1024

RGLRU_TM = 1024
FFN_TM = 1024
FFN_TF = 256
QKV_TM = 1024
ATTN_TILES = 64

BF16 = jnp.bfloat16
F32 = jnp.float32


def _rmsnorm(x, g):
    ms = jnp.mean(x * x, axis=-1, keepdims=True)
    return x * lax.rsqrt(ms + RMS_EPS) * g


def _dot(a, b):
    return jnp.dot(a, b, preferred_element_type=F32)


def _gelu_tanh(x):
    inner = x * (GELU_C0 + (GELU_C0 * 0.044715) * (x * x))
    hx = 0.5 * x
    return hx + hx * jnp.tanh(inner)


def _softplus(z):
    return jnp.maximum(z, 0.0) + jnp.log(1.0 + jnp.exp(-jnp.abs(z)))


def _rglru_kernel(x_ref, g_ref, win_ref, cw_ref, cb_ref, wr_ref, br_ref, wi_ref, bi_ref,
                  lam_ref, wout_ref, o_ref, xpad_ref, a_ref, u_ref, hs_ref, h_ref):
    tm, d = x_ref.shape
    bw = d // RG_BLOCKS
    pad = SUBLANES

    @pl.when(pl.program_id(1) == 0)
    def _():
        xpad_ref[0:pad, :] = jnp.zeros((pad, d), F32)
        h_ref[...] = jnp.zeros_like(h_ref)

    half_rate = (0.5 * RG_C) * -_softplus(-lam_ref[...])

    gates = []
    h = h_ref[...]
    for c0 in range(0, tm, ROW_CHUNK):
        rows = slice(c0, c0 + ROW_CHUNK)
        hb = _rmsnorm(x_ref[rows, :], g_ref[...]).astype(BF16)
        xpad_ref[pad + c0:pad + c0 + ROW_CHUNK, :] = _dot(hb, win_ref[:, d:])

        xc = cb_ref[...] + cw_ref[CONV_W - 1:CONV_W, :] * xpad_ref[pad + c0:pad + c0 + ROW_CHUNK, :]
        for k in range(CONV_W - 1):
            off = pad + c0 - (CONV_W - 1) + k
            xc = xc + cw_ref[k:k + 1, :] * xpad_ref[off:off + ROW_CHUNK, :]

        xcb = xc.astype(BF16)
        for n in range(RG_BLOCKS):
            sl = slice(n * bw, (n + 1) * bw)
            tr = jnp.tanh(_dot(xcb[:, sl], wr_ref[n]) + br_ref[:, sl])
            ti = jnp.tanh(_dot(xcb[:, sl], wi_ref[n]) + bi_ref[:, sl])
            log_a = tr * half_rate[:, sl] + half_rate[:, sl]
            a = jnp.exp(log_a)
            mult = jnp.exp2(jnp.log(jnp.maximum(jnp.tanh(log_a) * (-1.0 - a * a), 0.0))
                            * (0.5 * LOG2E))
            a_ref[rows, sl] = a
            u_ref[rows, sl] = (mult * xc[:, sl]) * (0.5 * ti + 0.5)

        for t in range(c0, c0 + ROW_CHUNK):
            h = a_ref[t:t + 1, :] * h + u_ref[t:t + 1, :]
            hs_ref[t:t + 1, :] = h

        gates.append(_gelu_tanh(_dot(hb, win_ref[:, :d])))
    h_ref[...] = h
    xpad_ref[0:pad, :] = xpad_ref[tm:tm + pad, :]

    for c, c0 in enumerate(range(0, tm, ROW_CHUNK)):
        rows = slice(c0, c0 + ROW_CHUNK)
        y = (hs_ref[rows, :] * gates[c]).astype(BF16)
        o_ref[rows, :] = x_ref[rows, :] + _dot(y, wout_ref[...])


def _rglru_mixer(x, g, w_in, conv_w, conv_b, w_r, b_r, w_i, b_i, lam, w_out, *, tm):
    bsz, s, d = x.shape
    assert s % tm == 0 and tm % ROW_CHUNK == 0
    bw = d // RG_BLOCKS
    const2 = lambda b, i: (0, 0)
    const3 = lambda b, i: (0, 0, 0)
    vec = pl.BlockSpec((1, d), const2)
    return pl.pallas_call(
        _rglru_kernel,
        grid=(bsz, s // tm),
        in_specs=[
            pl.BlockSpec((None, tm, d), lambda b, i: (b, i, 0)),
            vec,
            pl.BlockSpec((d, 2 * d), const2),
            pl.BlockSpec((CONV_W, d), const2),
            vec,
            pl.BlockSpec((RG_BLOCKS, bw, bw), const3),
            vec,
            pl.BlockSpec((RG_BLOCKS, bw, bw), const3),
            vec,
            vec,
            pl.BlockSpec((d, d), const2),
        ],
        out_specs=pl.BlockSpec((None, tm, d), lambda b, i: (b, i, 0)),
        out_shape=jax.ShapeDtypeStruct((bsz, s, d), F32),
        scratch_shapes=[
            pltpu.VMEM((tm + SUBLANES, d), F32),
            pltpu.VMEM((tm, d), F32),
            pltpu.VMEM((tm, d), F32),
            pltpu.VMEM((tm, d), F32),
            pltpu.VMEM((1, d), F32),
        ],
        compiler_params=pltpu.CompilerParams(
            dimension_semantics=("arbitrary", "arbitrary"),
            vmem_limit_bytes=VMEM_LIMIT),
        name="rglru_mixer",
    )(x, g, w_in, conv_w, conv_b, w_r, b_r, w_i, b_i, lam, w_out)


def _ffn_kernel(*refs, pre_proj, final_norm, nchunk):
    refs = list(refs)
    x_ref = refs.pop(0)
    if pre_proj:
        att_ref = refs.pop(0)
        wo_ref = refs.pop(0)
    g_ref, wg_ref, wu_ref, wd_ref = refs[:4]
    refs = refs[4:]
    if final_norm:
        fg_ref = refs.pop(0)
    o_ref, wg_s, wu_s, wd_s, hb_s = refs

    t = pl.program_id(0)

    def start_tile():
        x = x_ref[...]
        if pre_proj:
            x = x + _dot(att_ref[...], wo_ref[...].astype(BF16))
        o_ref[...] = x
        return _rmsnorm(x, g_ref[...]).astype(BF16)

    def add_chunk(h, c):
        gate = _dot(h, wg_s[c])
        up = _dot(h, wu_s[c])
        act = (gate * jax.nn.sigmoid(gate) * up).astype(BF16)
        o_ref[...] += _dot(act, wd_s[c])

    def finish_tile():
        if final_norm:
            o_ref[...] = _rmsnorm(o_ref[...], fg_ref[...])

    @pl.when(t < nchunk)
    def _():
        wg_s[t] = wg_ref[...].astype(BF16)
        wu_s[t] = wu_ref[...].astype(BF16)
        wd_s[t] = wd_ref[...].astype(BF16)

        @pl.when(t == 0)
        def _():
            hb_s[...] = start_tile()

        add_chunk(hb_s[...], t)

        @pl.when(t == nchunk - 1)
        def _():
            finish_tile()

    @pl.when(t >= nchunk)
    def _():
        h = start_tile()
        for c in range(nchunk):
            add_chunk(h, c)
        finish_tile()


def _ffn(x, g, w_gate, w_up, w_down, layer, *, att=None, w_o=None, final_g=None, tm, tf):
    m, d = x.shape
    dff = w_gate.shape[2]
    assert dff % tf == 0 and m % tm == 0
    nchunk = dff // tf
    pre_proj = att is not None
    final_norm = final_g is not None
    resident = dict(pipeline_mode=pl.Buffered(1))
    tile_of = lambda t: jnp.maximum(t - (nchunk - 1), 0)
    chunk_of = lambda t: jnp.minimum(t, nchunk - 1)
    row = pl.BlockSpec((tm, d), lambda t: (tile_of(t), 0))
    vec = pl.BlockSpec((1, d), lambda t: (0, 0), **resident)
    args, specs = [x], [row]
    if pre_proj:
        args += [att, w_o]
        specs += [row, pl.BlockSpec((d, d), lambda t: (0, 0), **resident)]
    args += [g, w_gate, w_up, w_down]
    specs += [vec,
              pl.BlockSpec((None, d, tf), lambda t: (layer, 0, chunk_of(t))),
              pl.BlockSpec((None, d, tf), lambda t: (layer, 0, chunk_of(t))),
              pl.BlockSpec((None, tf, d), lambda t: (layer, chunk_of(t), 0))]
    if final_norm:
        args.append(final_g)
        specs.append(vec)
    return pl.pallas_call(
        functools.partial(_ffn_kernel, pre_proj=pre_proj, final_norm=final_norm, nchunk=nchunk),
        grid=(nchunk + m // tm - 1,),
        in_specs=specs,
        out_specs=row,
        out_shape=jax.ShapeDtypeStruct((m, d), F32),
        scratch_shapes=[
            pltpu.VMEM((nchunk, d, tf), BF16),
            pltpu.VMEM((nchunk, d, tf), BF16),
            pltpu.VMEM((nchunk, tf, d), BF16),
            pltpu.VMEM((tm, d), BF16),
        ],
        compiler_params=pltpu.CompilerParams(
            dimension_semantics=("arbitrary",),
            vmem_limit_bytes=VMEM_LIMIT),
        name="ffn_attn_out" if pre_proj else "ffn",
    )(*args)


def _qkv_kernel(x_ref, g_ref, w_ref, o_ref, w_s, *, q_scale, nload, pad_tiles, tiles_per_seq):
    d = x_ref.shape[1]
    t = pl.program_id(0)

    @pl.when(t < nload)
    def _():
        w_s[t] = w_ref[...].astype(BF16)

    item = jnp.maximum(t - max(nload - pad_tiles, 0), 0)
    is_pad = lax.rem(item, tiles_per_seq) < pad_tiles

    @pl.when(is_pad)
    def _():
        o_ref[...] = jnp.zeros_like(o_ref)

    @pl.when(jnp.logical_not(is_pad))
    def _():
        half = x_ref.shape[0] // 2
        for r0 in (0, half):
            h = _rmsnorm(x_ref[r0:r0 + half, :], g_ref[...]).astype(BF16)
            for c in range(nload):
                y = _dot(h, w_s[c])
                if c == 0:
                    y = y * q_scale
                o_ref[r0:r0 + half, c * d:(c + 1) * d] = y.astype(o_ref.dtype)


def _qkv_proj(x, g, w_qkv, *, tm, pad):
    bsz, s, d = x.shape
    n = w_qkv.shape[1]
    nload = n // d
    pad_tiles = pad // tm
    assert pad % tm == 0 and s % tm == 0 and pad_tiles >= 1
    tiles_per_seq = s // tm + pad_tiles
    resident = dict(pipeline_mode=pl.Buffered(1))
    lead = max(nload - pad_tiles, 0)
    item = lambda t: jnp.maximum(t - lead, 0)
    seq_of = lambda t: item(t) // tiles_per_seq
    tile_of = lambda t: item(t) % tiles_per_seq
    return pl.pallas_call(
        functools.partial(_qkv_kernel, q_scale=LOG2E / math.sqrt(SB_HEAD_DIM), nload=nload,
                          pad_tiles=pad_tiles, tiles_per_seq=tiles_per_seq),
        grid=(lead + bsz * tiles_per_seq,),
        in_specs=[
            pl.BlockSpec((None, tm, d),
                         lambda t: (seq_of(t), jnp.maximum(tile_of(t) - pad_tiles, 0), 0)),
            pl.BlockSpec((1, d), lambda t: (0, 0), **resident),
            pl.BlockSpec((d, d), lambda t: (0, jnp.minimum(t, nload - 1))),
        ],
        out_specs=pl.BlockSpec((None, tm, n), lambda t: (seq_of(t), tile_of(t), 0)),
        out_shape=jax.ShapeDtypeStruct((bsz, s + pad, n), BF16),
        scratch_shapes=[pltpu.VMEM((nload, d, d), BF16)],
        compiler_params=pltpu.CompilerParams(
            dimension_semantics=("arbitrary",),
            vmem_limit_bytes=VMEM_LIMIT),
        name="qkv_proj",
    )(x, g, w_qkv)


def _sb_attn_kernel(q_ref, k_ref, v_ref, tri_ref, bias_ref, o_ref, acc_ref, c_ref, *, tiles, pad):
    tq, tk = Q_TILE, KEY_BLOCK
    step = pl.program_id(2)

    lane = lax.broadcasted_iota(jnp.int32, (tq, LANES), 1)
    head0, head1 = lane < SB_HEAD_DIM, lane >= SB_HEAD_DIM

    def scores(q, r0, first):
        zq = jnp.zeros_like(q)
        qs = jnp.concatenate([jnp.where(head0, q, zq), jnp.where(head1, q, zq)], axis=0)
        k = k_ref[pl.ds(r0, tk), :]
        z = lax.dot_general(qs, k, (((1,), (1,)), ((), ())), preferred_element_type=F32)
        if first:
            z = jnp.concatenate([z[:, :tk - LANES], z[:, tk - LANES:] + bias_ref[...]], axis=1)
        return z

    def weights(z):
        lg = jnp.log(1.0 + jnp.exp2(jnp.minimum(z, -z))) * LOG2E
        sp = jnp.maximum(z, 0.0) + lg
        sfx = _dot(sp.astype(BF16), tri_ref[...])
        w = jnp.exp2((z - sp) + sfx)
        return w.astype(BF16), sfx, sp

    def weighted_values(w, r0, row_scale=None):
        pv = _dot(w, v_ref[pl.ds(r0, tk), :])
        if row_scale is not None:
            pv = row_scale * pv
        return jnp.where(head0, pv[:tq], pv[tq:])

    rows2 = 2 * tq
    ends = [pl.multiple_of((step * tiles + g + 1) * tq + pad, tq) for g in range(tiles)]
    w, sfx, _ = weights(jnp.concatenate(
        [scores(q_ref[g * tq:(g + 1) * tq, :], ends[g] - tk, True) for g in range(tiles)],
        axis=0))
    for g in range(tiles):
        acc = weighted_values(w[g * rows2:(g + 1) * rows2], ends[g] - tk)
        o_ref[g * tq:(g + 1) * tq, :] = acc.astype(o_ref.dtype)

    bound = jnp.max(sfx[:, :LANES], axis=0, keepdims=True)[0:1, 0:1]

    @pl.when(bound[0, 0] > EXIT_LOG)
    def _():
        def more(carry):
            end, cm = carry
            return (end > pad) & (cm > EXIT_LOG)

        def walk(g, _):
            q0 = pl.multiple_of(g * tq, tq)
            q = q_ref[pl.ds(q0, tq), :]
            first_start = (step * tiles + g + 1) * tq + pad - tk
            w, sfx, sp = weights(scores(q, pl.multiple_of(first_start, tq), True))
            acc_ref[...] = weighted_values(w, pl.multiple_of(first_start, tq))
            c = sfx[:, 0:1] - sp[:, 0:1]
            c_ref[...] = c

            def body(carry):
                end, _ = carry
                r0 = pl.multiple_of(end - tk, tq)
                w, sfx, sp = weights(scores(q, r0, False))
                c = c_ref[...]
                acc_ref[...] += weighted_values(w, r0, jnp.exp2(c))
                c = c + (sfx[:, 0:1] - sp[:, 0:1])
                c_ref[...] = c
                return end - tk, jnp.max(c)

            lax.while_loop(more, body, (first_start, jnp.max(c)))
            o_ref[pl.ds(q0, tq), :] = acc_ref[...].astype(o_ref.dtype)
            return 0

        lax.fori_loop(0, tiles, walk, 0)


def _sb_attention(qkv, *, d, pad, tiles):
    bsz, s_pad, _ = qkv.shape
    s = s_pad - pad
    pairs = d // LANES
    rows = tiles * Q_TILE
    assert pad % rows == 0 and s % rows == 0
    tq, tk = Q_TILE, KEY_BLOCK
    kr = lax.broadcasted_iota(jnp.int32, (tk, tk), 0)
    kc = lax.broadcasted_iota(jnp.int32, (tk, tk), 1)
    tri = jnp.where(kr > kc, -1.0, 0.0).astype(BF16)
    assert tq <= LANES and tk % LANES == 0
    mr = lax.broadcasted_iota(jnp.int32, (2 * tq, LANES), 0) & (tq - 1)
    mc = lax.broadcasted_iota(jnp.int32, (2 * tq, LANES), 1)
    bias = jnp.where(mc - (LANES - tq) < mr, 0.0, MASKED).astype(F32)
    const = lambda b, p, i: (0, 0)
    return pl.pallas_call(
        functools.partial(_sb_attn_kernel, tiles=tiles, pad=pad),
        grid=(bsz, pairs, s // rows),
        in_specs=[
            pl.BlockSpec((None, rows, LANES), lambda b, p, i: (b, i + pad // rows, p)),
            pl.BlockSpec((None, s_pad, LANES), lambda b, p, i: (b, 0, pairs + p)),
            pl.BlockSpec((None, s_pad, LANES), lambda b, p, i: (b, 0, 2 * pairs + p)),
            pl.BlockSpec((tk, tk), const),
            pl.BlockSpec((2 * tq, LANES), const),
        ],
        out_specs=pl.BlockSpec((None, rows, LANES), lambda b, p, i: (b, i, p)),
        out_shape=jax.ShapeDtypeStruct((bsz, s, d), BF16),
        scratch_shapes=[
            pltpu.VMEM((Q_TILE, LANES), F32),
            pltpu.VMEM((2 * Q_TILE, 1), F32),
        ],
        compiler_params=pltpu.CompilerParams(
            dimension_semantics=("arbitrary", "arbitrary", "arbitrary"),
            vmem_limit_bytes=VMEM_LIMIT),
        name="sb_attention",
    )(qkv, qkv, qkv, tri, bias)


def kernel(x, norm_mix_g, norm_ffn_g, a_w_in, a_conv_w, a_conv_b, a_w_r, a_b_r, a_w_i, a_b_i,
           a_lambda, a_w_out, b_w_qkv, b_w_out, ffn_w_gate, ffn_w_up, ffn_w_down, final_g):
    bsz, s, d = x.shape
    m = bsz * s
    row = lambda v: v.reshape(1, -1)
    bf = lambda w: w.astype(BF16)

    x = _rglru_mixer(x, row(norm_mix_g[0]), bf(a_w_in[0]), a_conv_w[0], row(a_conv_b[0]),
                     bf(0.5 * a_w_r[0]), row(0.5 * a_b_r[0]), bf(0.5 * a_w_i[0]),
                     row(0.5 * a_b_i[0]), row(a_lambda[0]), bf(a_w_out[0]), tm=RGLRU_TM)
    x = _ffn(x.reshape(m, d), row(norm_ffn_g[0]), ffn_w_gate, ffn_w_up, ffn_w_down, 0,
             tm=FFN_TM, tf=FFN_TF)

    pad = ATTN_TILES * Q_TILE
    qkv = _qkv_proj(x.reshape(bsz, s, d), row(norm_mix_g[1]), b_w_qkv[0], tm=QKV_TM, pad=pad)
    att = _sb_attention(qkv, d=d, pad=pad, tiles=ATTN_TILES)
    out = _ffn(x, row(norm_ffn_g[1]), ffn_w_gate, ffn_w_up, ffn_w_down, 1,
               att=att.reshape(m, d), w_o=b_w_out[0], final_g=row(final_g), tm=FFN_TM, tf=FFN_TF)
    return out.reshape(bsz, s, d)
```
